```python
import jax, jax.numpy as jnp
from jax import lax
import numpy as np

D_MODEL = 1024
BATCH = 8
SEQ = 4096
DEPTH = 2

CTX_LEN = 256
GRID_W = 64
Q_BLOCK = 128
ROPE_THETA = 10000.0
EPS = 1e-6

MLA_HEADS = 6
MLA_Q_RANK = 256
MLA_KV_RANK = 128
MLA_NOPE = 64
MLA_ROPE = 32
MLA_V = 64
MLA_WIDTH = MLA_HEADS * MLA_V
GQA_HEADS = 6
GQA_KV_HEADS = 2
GQA_HEAD_DIM = 64
GQA_WIDTH = GQA_HEADS * GQA_HEAD_DIM
CONV_CH = 256
CONV_K = 31
MIX_WIDTH = MLA_WIDTH + GQA_WIDTH + CONV_CH

L_MLA_KV = 0
L_MLA_KR = L_MLA_KV + MLA_KV_RANK
L_GQA_K = L_MLA_KR + MLA_ROPE
L_GQA_V = L_GQA_K + GQA_KV_HEADS * GQA_HEAD_DIM
KV_COLS = L_GQA_V + GQA_KV_HEADS * GQA_HEAD_DIM
OFF_MLA_Q = 0
OFF_KV = OFF_MLA_Q + MLA_Q_RANK
OFF_GQA_Q = OFF_KV + KV_COLS
OFF_CONV = OFF_GQA_Q + GQA_WIDTH
OFF_GATE = OFF_CONV + 2 * CONV_CH
IN_COLS = OFF_GATE + MIX_WIDTH

kernel_name = "hybrid_mla_gqa_conformer_dit"


def rms_norm(x, g):
    xf = x.astype(jnp.float32)
    y = xf * lax.rsqrt(jnp.mean(xf * xf, axis=-1, keepdims=True) + EPS)
    return (y * g.astype(jnp.float32)).astype(x.dtype)


def layer_norm(x, g, b):
    xf = x.astype(jnp.float32)
    mu = jnp.mean(xf, axis=-1, keepdims=True)
    var = jnp.mean(jnp.square(xf - mu), axis=-1, keepdims=True)
    y = (xf - mu) * lax.rsqrt(var + EPS) * g.astype(jnp.float32) + b.astype(jnp.float32)
    return y.astype(x.dtype)


def rope_1d(x, pos):
    half = x.shape[-1] // 2
    freqs = ROPE_THETA ** (-jnp.arange(half, dtype=jnp.float32) / half)
    ang = pos.astype(jnp.float32)[:, None] * freqs[None, :]
    cos = jnp.cos(ang)[None, :, None, :]
    sin = jnp.sin(ang)[None, :, None, :]
    xf = x.astype(jnp.float32)
    x1, x2 = xf[..., :half], xf[..., half:]
    return jnp.concatenate([x1 * cos - x2 * sin, x1 * sin + x2 * cos], axis=-1).astype(x.dtype)


def axial_rope(x, row, col):
    r = x.shape[-1] // 2
    return jnp.concatenate([rope_1d(x[..., :r], row), rope_1d(x[..., r:], col)], axis=-1)


def block_attention(q, k, v):
    B, S, H, D = q.shape
    G = k.shape[2]
    R = H // G
    Dv = v.shape[-1]
    scale = D ** -0.5
    nblk = S // Q_BLOCK
    qb = q.reshape(B, nblk, Q_BLOCK, G, R, D).transpose(1, 0, 2, 3, 4, 5)

    def one_block(qi):
        s = jnp.einsum('bqgrd,bkgd->bgrqk', qi, k).astype(jnp.float32) * scale
        p = jax.nn.softmax(s, axis=-1).astype(v.dtype)
        return jnp.einsum('bgrqk,bkgv->bqgrv', p, v)

    o = lax.map(one_block, qb)
    return o.transpose(1, 0, 2, 3, 4, 5).reshape(B, S, H * Dv)


def mla_query(p, q_norm, w_uq, row, col):
    B, T = p.shape[:2]
    cq = rms_norm(p[..., OFF_MLA_Q:OFF_MLA_Q + MLA_Q_RANK], q_norm)
    q = (cq @ w_uq).reshape(B, T, MLA_HEADS, MLA_NOPE + MLA_ROPE)
    if row is not None:
        q = jnp.concatenate([q[..., :MLA_NOPE], axial_rope(q[..., MLA_NOPE:], row, col)], axis=-1)
    return q


def mla_keys_values(pkv, kv_norm, w_ukv, row, col):
    B, T = pkv.shape[:2]
    ckv = rms_norm(pkv[..., L_MLA_KV:L_MLA_KR], kv_norm)
    kv = (ckv @ w_ukv).reshape(B, T, MLA_HEADS, MLA_NOPE + MLA_V)
    k_nope, v = kv[..., :MLA_NOPE], kv[..., MLA_NOPE:]
    k_rope = pkv[..., L_MLA_KR:L_GQA_K][:, :, None, :]
    if row is not None:
        k_rope = axial_rope(k_rope, row, col)
    k_rope = jnp.broadcast_to(k_rope, (B, T, MLA_HEADS, MLA_ROPE))
    return jnp.concatenate([k_nope, k_rope], axis=-1), v


def gqa_query(p, q_norm, row, col):
    B, T = p.shape[:2]
    q = rms_norm(p[..., OFF_GQA_Q:OFF_CONV].reshape(B, T, GQA_HEADS, GQA_HEAD_DIM), q_norm)
    if row is not None:
        q = axial_rope(q, row, col)
    return q


def gqa_keys_values(pkv, k_norm, row, col):
    B, T = pkv.shape[:2]
    k = rms_norm(pkv[..., L_GQA_K:L_GQA_V].reshape(B, T, GQA_KV_HEADS, GQA_HEAD_DIM), k_norm)
    if row is not None:
        k = axial_rope(k, row, col)
    v = pkv[..., L_GQA_V:KV_COLS].reshape(B, T, GQA_KV_HEADS, GQA_HEAD_DIM)
    return k, v


def conformer_conv(u, dw_w, dw_b, ln_w, ln_b, pw_w, pw_b):
    a, g = jnp.split(u, 2, axis=-1)
    y = a * jax.nn.sigmoid(g)
    y = lax.conv_general_dilated(
        y, dw_w[:, None, :], window_strides=(1,),
        padding=((CONV_K // 2, CONV_K // 2),),
        dimension_numbers=('NWC', 'WIO', 'NWC'),
        feature_group_count=CONV_CH) + dw_b
    y = jax.nn.silu(layer_norm(y, ln_w, ln_b))
    return y @ pw_w + pw_b


def hybrid_layer(x, ctx, c, c_ctx, row, col, last,
                 norm_w, w_mod, b_mod, w_in, mla_q_norm, mla_w_uq, mla_kv_norm, mla_w_ukv,
                 gqa_q_norm, gqa_k_norm, conv_dw_w, conv_dw_b, conv_ln_w, conv_ln_b,
                 conv_pw_w, conv_pw_b, w_out):
    shift, scale, gate = jnp.split(jax.nn.silu(c) @ w_mod + b_mod, 3, axis=-1)
    shift_c, scale_c, gate_c = jnp.split(jax.nn.silu(c_ctx) @ w_mod + b_mod, 3, axis=-1)
    h = rms_norm(x, norm_w) * (1.0 + scale[:, None, :]) + shift[:, None, :]
    hc = rms_norm(ctx, norm_w) * (1.0 + scale_c) + shift_c

    p = h @ w_in
    if last:
        pc_kv = hc @ w_in[:, OFF_KV:OFF_GQA_Q]
    else:
        pc = hc @ w_in
        pc_kv = pc[..., OFF_KV:OFF_GQA_Q]
    p_kv = p[..., OFF_KV:OFF_GQA_Q]

    k_mla_c, v_mla_c = mla_keys_values(pc_kv, mla_kv_norm, mla_w_ukv, None, None)
    k_gqa_c, v_gqa_c = gqa_keys_values(pc_kv, gqa_k_norm, None, None)

    k_mla, v_mla = mla_keys_values(p_kv, mla_kv_norm, mla_w_ukv, row, col)
    q_mla = mla_query(p, mla_q_norm, mla_w_uq, row, col)
    o_mla = block_attention(q_mla, jnp.concatenate([k_mla, k_mla_c], axis=1),
                            jnp.concatenate([v_mla, v_mla_c], axis=1))
    k_gqa, v_gqa = gqa_keys_values(p_kv, gqa_k_norm, row, col)
    q_gqa = gqa_query(p, gqa_q_norm, row, col)
    o_gqa = block_attention(q_gqa, jnp.concatenate([k_gqa, k_gqa_c], axis=1),
                            jnp.concatenate([v_gqa, v_gqa_c], axis=1))
    o_conv = conformer_conv(p[..., OFF_CONV:OFF_GATE], conv_dw_w, conv_dw_b,
                            conv_ln_w, conv_ln_b, conv_pw_w, conv_pw_b)
    y = jnp.concatenate([o_mla, o_gqa, o_conv], axis=-1) * jax.nn.silu(p[..., OFF_GATE:])
    x_new = x + gate[:, None, :] * (y @ w_out)

    if last:
        return x_new, ctx

    oc_mla = block_attention(mla_query(pc, mla_q_norm, mla_w_uq, None, None), k_mla_c, v_mla_c)
    oc_gqa = block_attention(gqa_query(pc, gqa_q_norm, None, None), k_gqa_c, v_gqa_c)
    oc_conv = conformer_conv(pc[..., OFF_CONV:OFF_GATE], conv_dw_w, conv_dw_b,
                             conv_ln_w, conv_ln_b, conv_pw_w, conv_pw_b)
    yc = jnp.concatenate([oc_mla, oc_gqa, oc_conv], axis=-1) * jax.nn.silu(pc[..., OFF_GATE:])
    ctx_new = ctx + gate_c * (yc @ w_out)
    return x_new, ctx_new


def setup_inputs(seed: int = 0) -> dict:
    key = jax.random.key(seed)
    ks = jax.random.split(key, 24)
    f32 = jnp.float32
    nrm = lambda k, shape, s: jax.random.normal(k, shape, f32) * s
    L, D = DEPTH, D_MODEL
    return {
        'x': nrm(ks[0], (BATCH, SEQ, D), 1.0),
        'c': nrm(ks[1], (BATCH, D), 1.0),
        'ctx': nrm(ks[2], (BATCH, CTX_LEN, D), 1.0),
        'c_ctx': nrm(ks[3], (D,), 1.0),
        'norm_w': 1.0 + nrm(ks[4], (L, D), 0.05),
        'w_mod': nrm(ks[5], (L, D, 3 * D), 0.5 * D ** -0.5),
        'b_mod': nrm(ks[6], (L, 3 * D), 0.02),
        'w_in': nrm(ks[7], (L, D, IN_COLS), D ** -0.5),
        'mla_q_norm': 1.0 + nrm(ks[8], (L, MLA_Q_RANK), 0.05),
        'mla_w_uq': nrm(ks[9], (L, MLA_Q_RANK, MLA_HEADS * (MLA_NOPE + MLA_ROPE)), MLA_Q_RANK ** -0.5),
        'mla_kv_norm': 1.0 + nrm(ks[10], (L, MLA_KV_RANK), 0.05),
        'mla_w_ukv': nrm(ks[11], (L, MLA_KV_RANK, MLA_HEADS * (MLA_NOPE + MLA_V)), MLA_KV_RANK ** -0.5),
        'gqa_q_norm': 1.0 + nrm(ks[12], (L, GQA_HEAD_DIM), 0.05),
        'gqa_k_norm': 1.0 + nrm(ks[13], (L, GQA_HEAD_DIM), 0.05),
        'conv_dw_w': nrm(ks[14], (L, CONV_K, CONV_CH), CONV_K ** -0.5),
        'conv_dw_b': nrm(ks[15], (L, CONV_CH), 0.02),
        'conv_ln_w': 1.0 + nrm(ks[16], (L, CONV_CH), 0.05),
        'conv_ln_b': nrm(ks[17], (L, CONV_CH), 0.02),
        'conv_pw_w': nrm(ks[18], (L, CONV_CH, CONV_CH), CONV_CH ** -0.5),
        'conv_pw_b': nrm(ks[19], (L, CONV_CH), 0.02),
        'w_out': nrm(ks[20], (L, MIX_WIDTH, D), MIX_WIDTH ** -0.5),
        'final_norm_w': 1.0 + nrm(ks[21], (D,), 0.05),
    }


def reference(x, c, ctx, c_ctx, norm_w, w_mod, b_mod, w_in, mla_q_norm, mla_w_uq, mla_kv_norm,
              mla_w_ukv, gqa_q_norm, gqa_k_norm, conv_dw_w, conv_dw_b, conv_ln_w, conv_ln_b,
              conv_pw_w, conv_pw_b, w_out, final_norm_w):
    S = x.shape[1]
    rows = S // GRID_W
    row = jnp.repeat(jnp.arange(rows, dtype=jnp.int32), GRID_W)
    col = jnp.tile(jnp.arange(GRID_W, dtype=jnp.int32), rows)
    for l in range(DEPTH):
        x, ctx = hybrid_layer(
            x, ctx, c, c_ctx, row, col, l == DEPTH - 1,
            norm_w[l], w_mod[l], b_mod[l], w_in[l], mla_q_norm[l], mla_w_uq[l],
            mla_kv_norm[l], mla_w_ukv[l], gqa_q_norm[l], gqa_k_norm[l],
            conv_dw_w[l], conv_dw_b[l], conv_ln_w[l], conv_ln_b[l],
            conv_pw_w[l], conv_pw_b[l], w_out[l])
    return rms_norm(x, final_norm_w)
```

```python
import functools
import math

import jax
import jax.numpy as jnp
import numpy as np
from jax import lax
from jax.experimental import pallas as pl
from jax.experimental.pallas import tpu as pltpu

F32 = jnp.float32
BF16 = jnp.bfloat16

LANES = 128
HALO_ROWS = 16
VMEM_LIMIT = 56 * 1024 * 1024

D_MODEL = 1024
GRID_W = 64
ROPE_THETA = 10000.0
EPS = 1e-6
LOG2E = math.log2(math.e)

MLA_HEADS = 6
MLA_Q_RANK = 256
MLA_KV_RANK = 128
MLA_NOPE = 64
MLA_ROPE = 32
MLA_V = 64
GQA_HEADS = 6
GQA_KV_HEADS = 2
GQA_HEAD_DIM = 64
CONV_CH = 256
CONV_K = 31
MIX_WIDTH = 1024

OFF_MLA_Q = 0
OFF_KV = 256
OFF_KR = OFF_KV + MLA_KV_RANK
OFF_GQA_K = OFF_KR + MLA_ROPE
OFF_GQA_V = OFF_GQA_K + 128
OFF_GQA_Q = OFF_GQA_V + 128
OFF_CONV = OFF_GQA_Q + 384
OFF_GATE = OFF_CONV + 2 * CONV_CH

P_CQ = 0
P_KV = 256
P_GK = 512
P_GQ = 896
P_CONV = 1664
P_GATE = 2176
P_COLS = 3200

GQA_SLOTS = (0, 3, 1, 4, 2, 5)

TS = 256
N_QH = 12
N_KH = 8
N_VH = 4
N_PAIRS = 6
KEY_CHUNK = 512


def _pad_cols(w, width):
    return jnp.pad(w, ((0, 0), (0, width - w.shape[1])))


def _layout_w_in(w_in):
    z64 = jnp.zeros((D_MODEL, 64), F32)
    cols = [w_in[:, OFF_MLA_Q:OFF_KV], w_in[:, OFF_KV:OFF_KR],
            z64, _pad_cols(w_in[:, OFF_KR:OFF_GQA_K], 64)]
    for g in range(GQA_KV_HEADS):
        cols.append(_pad_cols(w_in[:, OFF_GQA_K + 64 * g:OFF_GQA_K + 64 * (g + 1)], LANES))
    cols.append(w_in[:, OFF_GQA_V:OFF_GQA_Q])
    for h in GQA_SLOTS:
        cols.append(_pad_cols(w_in[:, OFF_GQA_Q + 64 * h:OFF_GQA_Q + 64 * (h + 1)], LANES))
    cols.append(w_in[:, OFF_CONV:OFF_GATE])
    cols.append(_mix_order(w_in[:, OFF_GATE:].T).T)
    out = jnp.concatenate(cols, axis=1)
    assert out.shape == (D_MODEL, P_COLS)
    return out.astype(BF16)


def _mix_order(rows):
    gqa = [rows[384 + 64 * h:384 + 64 * (h + 1)] for h in GQA_SLOTS]
    return jnp.concatenate([rows[:384]] + gqa + [rows[768:]], axis=0)


def _layout_w_uq(w_uq):
    cols = [_pad_cols(w_uq[:, 96 * h:96 * (h + 1)], LANES) for h in range(MLA_HEADS)]
    return jnp.concatenate(cols, axis=1).astype(BF16)


def _layout_w_ukv(w_ukv):
    k = [_pad_cols(w_ukv[:, 128 * h:128 * h + 64], LANES) for h in range(MLA_HEADS)]
    v = [w_ukv[:, 128 * h + 64:128 * (h + 1)] for h in range(MLA_HEADS)]
    return jnp.concatenate(k + v, axis=1).astype(BF16)


def _rope_tables(seq, ctx_len):
    t = jnp.arange(seq, dtype=jnp.int32)
    row = (t // GRID_W).astype(F32)[:, None]
    col = (t % GRID_W).astype(F32)[:, None]

    def seg_tables(width):
        r = width // 2
        half = r // 2
        freqs = ROPE_THETA ** (-jnp.arange(half, dtype=F32) / half)
        cs, ss = [], []
        for pos in (row, col):
            ang = pos * freqs[None, :]
            c, s = jnp.cos(ang), jnp.sin(ang)
            cs += [c, c]
            ss += [-s, s]
        return jnp.concatenate(cs, axis=1), jnp.concatenate(ss, axis=1)

    def full(c, s, lo):
        w = c.shape[1]
        cf = jnp.concatenate([jnp.ones((seq, lo), F32), c, jnp.ones((seq, LANES - lo - w), F32)], 1)
        sf = jnp.concatenate([jnp.zeros((seq, lo), F32), s, jnp.zeros((seq, LANES - lo - w), F32)], 1)
        cf = jnp.concatenate([cf, jnp.ones((ctx_len, LANES), F32)], 0)
        sf = jnp.concatenate([sf, jnp.zeros((ctx_len, LANES), F32)], 0)
        return cf, sf

    mc, ms = full(*seg_tables(MLA_ROPE), MLA_NOPE)
    gc, gs = full(*seg_tables(GQA_HEAD_DIM), 0)
    q_mla = (MLA_NOPE + MLA_ROPE) ** -0.5 * LOG2E
    q_gqa = GQA_HEAD_DIM ** -0.5 * LOG2E
    return jnp.stack([mc * q_mla, ms * q_mla, mc, ms, gc * q_gqa, gs * q_gqa, gc, gs])


def _mod_kernel(c_ref, w_ref, b_ref, o_ref):
    a = c_ref[...]
    a = a * jax.nn.sigmoid(a)
    o_ref[0] = jnp.dot(a, w_ref[0], preferred_element_type=F32,
                       precision=lax.Precision.HIGHEST) + b_ref[0]


def _modulation(c_all, w_mod, b_mod):
    depth = w_mod.shape[0]
    rows = c_all.shape[0]
    tn = 1024
    return pl.pallas_call(
        _mod_kernel,
        grid=(depth, 3 * D_MODEL // tn),
        in_specs=[pl.BlockSpec((rows, D_MODEL), lambda l, n: (0, 0)),
                  pl.BlockSpec((1, D_MODEL, tn), lambda l, n: (l, 0, n)),
                  pl.BlockSpec((1, 1, tn), lambda l, n: (l, 0, n))],
        out_specs=pl.BlockSpec((1, rows, tn), lambda l, n: (l, 0, n)),
        out_shape=jax.ShapeDtypeStruct((depth, rows, 3 * D_MODEL), F32),
        compiler_params=pltpu.CompilerParams(
            dimension_semantics=("arbitrary", "arbitrary"), vmem_limit_bytes=VMEM_LIMIT),
        name="adaln_mod",
    )(c_all, w_mod, b_mod.reshape(depth, 1, 3 * D_MODEL))


def _rms(x, gain, n):
    ss = jnp.sum(x * x, axis=-1, keepdims=True) * (1.0 / n)
    return x * lax.rsqrt(ss + EPS) * gain


def _rope(x, cos, sin, half):
    lane = lax.broadcasted_iota(jnp.int32, x.shape, 1)
    first = (lane % (2 * half)) < half
    partner = jnp.where(first, pltpu.roll(x, LANES - half, 1), pltpu.roll(x, half, 1))
    return x * cos + partner * sin


def _in_proj_kernel(x_ref, mod_ref, nw_ref, tab_ref, win_ref, wuq_ref, wukv_ref,
                    qn_ref, kvn_ref, gqn_ref, gkn_ref,
                    q_ref, k_ref, v_ref, glu_ref, sg_ref):
    x = x_ref[0]
    shift = mod_ref[0, :, 0:D_MODEL]
    scale = mod_ref[0, :, D_MODEL:2 * D_MODEL]
    h = _rms(x, nw_ref[...], D_MODEL) * (1.0 + scale) + shift
    hb = h.astype(BF16)

    def proj(lo, hi):
        return jnp.dot(hb, win_ref[:, lo:hi], preferred_element_type=F32)

    cq = _rms(proj(P_CQ, P_KV), qn_ref[...], MLA_Q_RANK)
    q = jnp.dot(cq.astype(BF16), wuq_ref[...], preferred_element_type=F32)
    for hd in range(MLA_HEADS):
        g = _rope(q[:, LANES * hd:LANES * (hd + 1)], tab_ref[0], tab_ref[1], MLA_ROPE // 4)
        q_ref[0, hd] = g.astype(BF16)

    pkv = proj(P_KV, P_GK)
    k_rope = _rope(pkv[:, LANES:], tab_ref[2], tab_ref[3], MLA_ROPE // 4)
    ckv = _rms(pkv[:, :LANES], kvn_ref[...], MLA_KV_RANK)
    kv = jnp.dot(ckv.astype(BF16), wukv_ref[...], preferred_element_type=F32)
    for hd in range(MLA_HEADS):
        k_ref[0, hd] = (kv[:, LANES * hd:LANES * (hd + 1)] + k_rope).astype(BF16)
    for j in range(MLA_HEADS // 2):
        lo = LANES * (MLA_HEADS + j)
        v_ref[0, j] = kv[:, lo:lo + LANES].astype(BF16)

    pg = proj(P_GK, P_GQ)
    for g in range(GQA_KV_HEADS):
        kk = _rms(pg[:, LANES * g:LANES * (g + 1)], gkn_ref[...], GQA_HEAD_DIM)
        k_ref[0, MLA_HEADS + g] = _rope(kk, tab_ref[6], tab_ref[7], GQA_HEAD_DIM // 4).astype(BF16)
    v_ref[0, MLA_HEADS // 2] = pg[:, 2 * LANES:].astype(BF16)

    pq = proj(P_GQ, P_CONV)
    for s in range(GQA_HEADS):
        qq = _rms(pq[:, LANES * s:LANES * (s + 1)], gqn_ref[...], GQA_HEAD_DIM)
        q_ref[0, MLA_HEADS + s] = _rope(qq, tab_ref[4], tab_ref[5], GQA_HEAD_DIM // 4).astype(BF16)

    pc = proj(P_CONV, P_GATE)
    glu_ref[0] = pc[:, :CONV_CH] * jax.nn.sigmoid(pc[:, CONV_CH:])
    gt = proj(P_GATE, P_COLS)
    sg_ref[0] = (gt * jax.nn.sigmoid(gt)).astype(BF16)


def _in_proj(xc, mod_l, n_lat, ctx_row, nw, tab, win, wuq, wukv, qn, kvn, gqn, gkn):
    batch, tt, _ = xc.shape
    nt = tt // TS

    def const(shape):
        return pl.BlockSpec(shape, lambda i, b: (0,) * len(shape))

    def heads(n):
        return pl.BlockSpec((1, n, TS, LANES), lambda i, b: (b, 0, i, 0))

    return pl.pallas_call(
        _in_proj_kernel,
        grid=(nt, batch),
        in_specs=[
            pl.BlockSpec((1, TS, D_MODEL), lambda i, b: (b, i, 0)),
            pl.BlockSpec((1, 1, 3 * D_MODEL), lambda i, b: (jnp.where(i < n_lat, b, ctx_row), 0, 0)),
            const((1, D_MODEL)),
            pl.BlockSpec((8, TS, LANES), lambda i, b: (0, i, 0)),
            const((D_MODEL, P_COLS)), const((MLA_Q_RANK, 768)), const((MLA_KV_RANK, 1152)),
            const((1, MLA_Q_RANK)), const((1, MLA_KV_RANK)), const((1, LANES)), const((1, LANES)),
        ],
        out_specs=[heads(N_QH), heads(N_KH), heads(N_VH),
                   pl.BlockSpec((1, TS, CONV_CH), lambda i, b: (b, i, 0)),
                   pl.BlockSpec((1, TS, MIX_WIDTH), lambda i, b: (b, i, 0))],
        out_shape=[jax.ShapeDtypeStruct((batch, N_QH, tt, LANES), BF16),
                   jax.ShapeDtypeStruct((batch, N_KH, tt, LANES), BF16),
                   jax.ShapeDtypeStruct((batch, N_VH, tt, LANES), BF16),
                   jax.ShapeDtypeStruct((batch, tt, CONV_CH), F32),
                   jax.ShapeDtypeStruct((batch, tt, MIX_WIDTH), BF16)],
        compiler_params=pltpu.CompilerParams(
            dimension_semantics=("arbitrary", "arbitrary"), vmem_limit_bytes=VMEM_LIMIT),
        name="in_proj",
    )(xc, mod_l, nw, tab, win, wuq, wukv, qn, kvn, gqn, gkn)


def _key_chunks(n_keys):
    chunks, lo = [], 0
    while lo < n_keys:
        size = min(KEY_CHUNK, n_keys - lo)
        chunks.append((lo, size))
        lo += size
    return chunks


def _attn_kernel(q_ref, k_ref, v_ref, o_ref, s_ref, p_ref, *, n_keys):
    chunks = _key_chunks(n_keys)

    def head(qi, ki, vi):
        q = q_ref[0, qi]
        m = jnp.full((TS, LANES), -jnp.inf, F32)
        for lo, size in chunks:
            sc = lax.dot_general(q, k_ref[0, ki, lo:lo + size, :], (((1,), (1,)), ((), ())),
                                 preferred_element_type=F32)
            s_ref[:, lo:lo + size] = sc
            for u in range(size // LANES):
                m = jnp.maximum(m, sc[:, LANES * u:LANES * (u + 1)])
        m_row = jnp.max(m, axis=-1, keepdims=True)
        l = jnp.zeros((TS, LANES), F32)
        for lo, size in chunks:
            pc = jnp.exp2(s_ref[:, lo:lo + size] - m_row)
            for u in range(size // LANES):
                l = l + pc[:, LANES * u:LANES * (u + 1)]
            p_ref[:, lo:lo + size] = pc.astype(BF16)
        o = jnp.dot(p_ref[...], v_ref[0, vi], preferred_element_type=F32)
        return o / jnp.sum(l, axis=-1, keepdims=True)

    lane = lax.broadcasted_iota(jnp.int32, (TS, LANES), 1)

    def pair(j, carry):
        mla = j < MLA_HEADS // 2
        oa = head(2 * j, jnp.where(mla, 2 * j, MLA_HEADS), jnp.minimum(j, MLA_HEADS // 2))
        ob = head(2 * j + 1, jnp.where(mla, 2 * j + 1, MLA_HEADS + 1), jnp.minimum(j, MLA_HEADS // 2))
        o_ref[0, j] = jnp.where(lane < 64, oa, ob).astype(BF16)
        return carry

    lax.fori_loop(0, N_PAIRS, pair, 0)


def _attention(q_all, k_all, v_all, q_tile0, n_q_tiles, key_block, n_keys):
    batch, _, tt, _ = q_all.shape
    return pl.pallas_call(
        functools.partial(_attn_kernel, n_keys=n_keys),
        grid=(batch, n_q_tiles),
        in_specs=[
            pl.BlockSpec((1, N_QH, TS, LANES), lambda b, i: (b, 0, q_tile0 + i, 0)),
            pl.BlockSpec((1, N_KH, n_keys, LANES), lambda b, i: (b, 0, key_block, 0)),
            pl.BlockSpec((1, N_VH, n_keys, LANES), lambda b, i: (b, 0, key_block, 0)),
        ],
        out_specs=pl.BlockSpec((1, N_PAIRS, TS, LANES), lambda b, i: (b, 0, i, 0)),
        out_shape=jax.ShapeDtypeStruct((batch, N_PAIRS, n_q_tiles * TS, LANES), BF16),
        scratch_shapes=[pltpu.VMEM((TS, n_keys), F32), pltpu.VMEM((TS, n_keys), BF16)],
        compiler_params=pltpu.CompilerParams(
            dimension_semantics=("arbitrary", "arbitrary"), vmem_limit_bytes=VMEM_LIMIT),
        name="attention_%d" % n_keys,
    )(q_all, k_all, v_all)


def _out_proj_kernel(o_ref, glu_ref, prev_ref, next_ref, sg_ref, x_ref, mod_ref,
                     dww_ref, dwb_ref, lnw_ref, lnb_ref, pww_ref, pwb_ref, wout_ref, fnw_ref,
                     out_ref, ext_ref, *, n_lat, final):
    i = pl.program_id(0)
    has_prev = jnp.logical_and(i != 0, i != n_lat)
    has_next = jnp.logical_and(i != n_lat - 1, i != pl.num_programs(0) - 1)
    ext_ref[0:HALO_ROWS, :] = jnp.where(has_prev, prev_ref[0], 0.0)
    ext_ref[HALO_ROWS:HALO_ROWS + TS, :] = glu_ref[0]
    ext_ref[HALO_ROWS + TS:, :] = jnp.where(has_next, next_ref[0], 0.0)

    acc = jnp.zeros((TS, CONV_CH), F32) + dwb_ref[...]
    for k in range(CONV_K):
        lo = HALO_ROWS - CONV_K // 2 + k
        acc = acc + ext_ref[lo:lo + TS, :] * dww_ref[k:k + 1, :]
    mu = jnp.mean(acc, axis=-1, keepdims=True)
    cen = acc - mu
    var = jnp.mean(cen * cen, axis=-1, keepdims=True)
    z = cen * lax.rsqrt(var + EPS) * lnw_ref[...] + lnb_ref[...]
    z = z * jax.nn.sigmoid(z)
    o_conv = jnp.dot(z.astype(BF16), pww_ref[...], preferred_element_type=F32) + pwb_ref[...]

    mix = jnp.concatenate([o_ref[0, j].astype(F32) for j in range(N_PAIRS)] + [o_conv], axis=1)
    y = (mix * sg_ref[0].astype(F32)).astype(BF16)
    r = jnp.dot(y, wout_ref[...], preferred_element_type=F32)
    x_new = x_ref[0] + mod_ref[0, :, 2 * D_MODEL:] * r
    if final:
        x_new = _rms(x_new, fnw_ref[...], D_MODEL)
    out_ref[0] = x_new


def _out_proj(o_all, glu, sg, xc, mod_l, n_lat, ctx_row, n_tiles, dww, dwb, lnw, lnb, pww, pwb, wout,
              fnw, final):
    batch, tt, _ = xc.shape
    hb = TS // HALO_ROWS
    last_halo = tt // HALO_ROWS - 1

    def const(shape):
        return pl.BlockSpec(shape, lambda i, b: (0,) * len(shape))

    tile = lambda w: pl.BlockSpec((1, TS, w), lambda i, b: (b, i, 0))
    return pl.pallas_call(
        functools.partial(_out_proj_kernel, n_lat=n_lat, final=final),
        grid=(n_tiles, batch),
        in_specs=[
            pl.BlockSpec((1, N_PAIRS, TS, LANES), lambda i, b: (b, 0, i, 0)),
            tile(CONV_CH),
            pl.BlockSpec((1, HALO_ROWS, CONV_CH), lambda i, b: (b, jnp.maximum(i * hb - 1, 0), 0)),
            pl.BlockSpec((1, HALO_ROWS, CONV_CH),
                         lambda i, b: (b, jnp.minimum((i + 1) * hb, last_halo), 0)),
            tile(MIX_WIDTH), tile(D_MODEL),
            pl.BlockSpec((1, 1, 3 * D_MODEL), lambda i, b: (jnp.where(i < n_lat, b, ctx_row), 0, 0)),
            const((CONV_K, CONV_CH)), const((1, CONV_CH)), const((1, CONV_CH)), const((1, CONV_CH)),
            const((CONV_CH, CONV_CH)), const((1, CONV_CH)), const((MIX_WIDTH, D_MODEL)),
            const((1, D_MODEL)),
        ],
        out_specs=tile(D_MODEL),
        out_shape=jax.ShapeDtypeStruct((batch, n_tiles * TS, D_MODEL), F32),
        scratch_shapes=[pltpu.VMEM((TS + 2 * HALO_ROWS, CONV_CH), F32)],
        compiler_params=pltpu.CompilerParams(
            dimension_semantics=("arbitrary", "arbitrary"), vmem_limit_bytes=VMEM_LIMIT),
        name="out_proj_final" if final else "out_proj",
    )(o_all, glu, glu, glu, sg, xc, mod_l, dww, dwb, lnw, lnb, pww, pwb, wout, fnw)


def kernel(x, c, ctx, c_ctx, norm_w, w_mod, b_mod, w_in, mla_q_norm, mla_w_uq, mla_kv_norm,
           mla_w_ukv, gqa_q_norm, gqa_k_norm, conv_dw_w, conv_dw_b, conv_ln_w, conv_ln_b,
           conv_pw_w, conv_pw_b, w_out, final_norm_w):
    batch, seq, _ = x.shape
    ctx_len = ctx.shape[1]
    depth = w_in.shape[0]
    assert seq % TS == 0 and ctx_len == TS and seq % GRID_W == 0
    n_lat = seq // TS
    tt = seq + ctx_len
    mod_rows = 16
    assert batch < mod_rows

    c_all = jnp.concatenate([c, c_ctx[None, :], jnp.zeros((mod_rows - batch - 1, D_MODEL), F32)], 0)
    mod = _modulation(c_all, w_mod, b_mod)
    tab = _rope_tables(seq, ctx_len)
    xc = jnp.concatenate([x, ctx], axis=1)
    row = lambda v: v.reshape(1, -1)
    dup = lambda v: jnp.concatenate([v, v]).reshape(1, LANES)

    for l in range(depth):
        final = l == depth - 1
        mod_l = mod[l].reshape(mod_rows, 1, 3 * D_MODEL)
        q_all, k_all, v_all, glu, sg = _in_proj(
            xc, mod_l, n_lat, batch, row(norm_w[l]), tab, _layout_w_in(w_in[l]),
            _layout_w_uq(mla_w_uq[l]), _layout_w_ukv(mla_w_ukv[l]), row(mla_q_norm[l]),
            row(mla_kv_norm[l]), dup(gqa_q_norm[l]), dup(gqa_k_norm[l]))
        o_lat = _attention(q_all, k_all, v_all, 0, n_lat, 0, tt)
        if final:
            o_all, n_tiles = o_lat, n_lat
        else:
            o_ctx = _attention(q_all, k_all, v_all, n_lat, 1, n_lat, ctx_len)
            o_all, n_tiles = jnp.concatenate([o_lat, o_ctx], axis=2), n_lat + 1
        xc = _out_proj(o_all, glu, sg, xc, mod_l, n_lat, batch, n_tiles,
                       conv_dw_w[l], row(conv_dw_b[l]), row(conv_ln_w[l]), row(conv_ln_b[l]),
                       conv_pw_w[l].astype(BF16), row(conv_pw_b[l]),
                       _mix_order(w_out[l]).astype(BF16), row(final_norm_w), final)
    return xc
```

```python
import functools
import math

import jax
import jax.numpy as jnp
import numpy as np
from jax import lax
from jax.experimental import pallas as pl
from jax.experimental.pallas import tpu as pltpu

F32 = jnp.float32
BF16 = jnp.bfloat16

LANES = 128
HALO_ROWS = 16
VMEM_LIMIT = 56 * 1024 * 1024

D_MODEL = 1024
GRID_W = 64
ROPE_THETA = 10000.0
EPS = 1e-6
LOG2E = math.log2(math.e)

MLA_HEADS = 6
MLA_Q_RANK = 256
MLA_KV_RANK = 128
MLA_NOPE = 64
MLA_ROPE = 32
MLA_V = 64
GQA_HEADS = 6
GQA_KV_HEADS = 2
GQA_HEAD_DIM = 64
CONV_CH = 256
CONV_K = 31
MIX_WIDTH = 1024

OFF_MLA_Q = 0
OFF_KV = 256
OFF_KR = OFF_KV + MLA_KV_RANK
OFF_GQA_K = OFF_KR + MLA_ROPE
OFF_GQA_V = OFF_GQA_K + 128
OFF_GQA_Q = OFF_GQA_V + 128
OFF_CONV = OFF_GQA_Q + 384
OFF_GATE = OFF_CONV + 2 * CONV_CH

P_CQ = 0
P_KV = 256
P_GK = 512
P_GQ = 896
P_CONV = 1664
P_GATE = 2176
P_COLS = 3200

GQA_SLOTS = (0, 3, 1, 4, 2, 5)

TS = 256
N_QH = 12
N_KH = 8
N_VH = 4
N_PAIRS = 6
TQ = 512
KEY_CHUNK = 512
CONV_SPAN = TS + 24


def _pad_cols(w, width):
    return jnp.pad(w, ((0, 0), (0, width - w.shape[1])))


def _layout_w_in(w_in):
    z64 = jnp.zeros((D_MODEL, 64), F32)
    cols = [w_in[:, OFF_MLA_Q:OFF_KV], w_in[:, OFF_KV:OFF_KR],
            z64, _pad_cols(w_in[:, OFF_KR:OFF_GQA_K], 64)]
    for g in range(GQA_KV_HEADS):
        cols.append(_pad_cols(w_in[:, OFF_GQA_K + 64 * g:OFF_GQA_K + 64 * (g + 1)], LANES))
    cols.append(w_in[:, OFF_GQA_V:OFF_GQA_Q])
    for h in GQA_SLOTS:
        cols.append(_pad_cols(w_in[:, OFF_GQA_Q + 64 * h:OFF_GQA_Q + 64 * (h + 1)], LANES))
    cols.append(w_in[:, OFF_CONV:OFF_GATE])
    cols.append(_mix_order(w_in[:, OFF_GATE:].T).T)
    out = jnp.concatenate(cols, axis=1)
    assert out.shape == (D_MODEL, P_COLS)
    return out.astype(BF16)


def _mix_order(rows):
    gqa = [rows[384 + 64 * h:384 + 64 * (h + 1)] for h in GQA_SLOTS]
    return jnp.concatenate([rows[:384]] + gqa + [rows[768:]], axis=0)


def _layout_w_uq(w_uq):
    cols = [_pad_cols(w_uq[:, 96 * h:96 * (h + 1)], LANES) for h in range(MLA_HEADS)]
    return jnp.concatenate(cols, axis=1).astype(BF16)


def _layout_w_ukv(w_ukv):
    k = [_pad_cols(w_ukv[:, 128 * h:128 * h + 64], LANES) for h in range(MLA_HEADS)]
    v = [w_ukv[:, 128 * h + 64:128 * (h + 1)] for h in range(MLA_HEADS)]
    return jnp.concatenate(k + v, axis=1).astype(BF16)


def _rope_tables(seq, ctx_len):
    t = jnp.arange(seq, dtype=jnp.int32)
    row = (t // GRID_W).astype(F32)[:, None]
    col = (t % GRID_W).astype(F32)[:, None]

    def seg_tables(width):
        r = width // 2
        half = r // 2
        freqs = ROPE_THETA ** (-jnp.arange(half, dtype=F32) / half)
        cs, ss = [], []
        for pos in (row, col):
            ang = pos * freqs[None, :]
            c, s = jnp.cos(ang), jnp.sin(ang)
            cs += [c, c]
            ss += [-s, s]
        return jnp.concatenate(cs, axis=1), jnp.concatenate(ss, axis=1)

    def full(c, s, lo):
        w = c.shape[1]
        cf = jnp.concatenate([jnp.ones((seq, lo), F32), c, jnp.ones((seq, LANES - lo - w), F32)], 1)
        sf = jnp.concatenate([jnp.zeros((seq, lo), F32), s, jnp.zeros((seq, LANES - lo - w), F32)], 1)
        cf = jnp.concatenate([cf, jnp.ones((ctx_len, LANES), F32)], 0)
        sf = jnp.concatenate([sf, jnp.zeros((ctx_len, LANES), F32)], 0)
        return cf, sf

    mc, ms = full(*seg_tables(MLA_ROPE), MLA_NOPE)
    gc, gs = full(*seg_tables(GQA_HEAD_DIM), 0)
    q_mla = (MLA_NOPE + MLA_ROPE) ** -0.5 * LOG2E
    q_gqa = GQA_HEAD_DIM ** -0.5 * LOG2E
    return jnp.stack([mc * q_mla, ms * q_mla, mc, ms, gc * q_gqa, gs * q_gqa, gc, gs])


def _mod_kernel(c_ref, w_ref, b_ref, o_ref):
    a = c_ref[...]
    a = a * jax.nn.sigmoid(a)
    o_ref[0] = jnp.dot(a, w_ref[0], preferred_element_type=F32,
                       precision=lax.Precision.HIGHEST) + b_ref[0]


def _modulation(c_all, w_mod, b_mod):
    depth = w_mod.shape[0]
    rows = c_all.shape[0]
    tn = 1024
    return pl.pallas_call(
        _mod_kernel,
        grid=(depth, 3 * D_MODEL // tn),
        in_specs=[pl.BlockSpec((rows, D_MODEL), lambda l, n: (0, 0)),
                  pl.BlockSpec((1, D_MODEL, tn), lambda l, n: (l, 0, n)),
                  pl.BlockSpec((1, 1, tn), lambda l, n: (l, 0, n))],
        out_specs=pl.BlockSpec((1, rows, tn), lambda l, n: (l, 0, n)),
        out_shape=jax.ShapeDtypeStruct((depth, rows, 3 * D_MODEL), F32),
        compiler_params=pltpu.CompilerParams(
            dimension_semantics=("arbitrary", "arbitrary"), vmem_limit_bytes=VMEM_LIMIT),
        name="adaln_mod",
    )(c_all, w_mod, b_mod.reshape(depth, 1, 3 * D_MODEL))


def _rms(x, gain, n):
    ss = jnp.sum(x * x, axis=-1, keepdims=True) * (1.0 / n)
    return x * lax.rsqrt(ss + EPS) * gain


def _rope(x, cos, sin, half):
    lane = lax.broadcasted_iota(jnp.int32, x.shape, 1)
    first = (lane % (2 * half)) < half
    partner = jnp.where(first, pltpu.roll(x, LANES - half, 1), pltpu.roll(x, half, 1))
    return x * cos + partner * sin


def _in_proj_kernel(x_ref, mod_ref, nw_ref, tab_ref, win_ref, wuq_ref, wukv_ref,
                    qn_ref, kvn_ref, gqn_ref, gkn_ref,
                    q_ref, k_ref, v_ref, glu_ref, sg_ref):
    x = x_ref[0]
    shift = mod_ref[0, :, 0:D_MODEL]
    scale = mod_ref[0, :, D_MODEL:2 * D_MODEL]
    h = _rms(x, nw_ref[...], D_MODEL) * (1.0 + scale) + shift
    hb = h.astype(BF16)

    def proj(lo, hi):
        return jnp.dot(hb, win_ref[:, lo:hi], preferred_element_type=F32)

    cq = _rms(proj(P_CQ, P_KV), qn_ref[...], MLA_Q_RANK)
    q = jnp.dot(cq.astype(BF16), wuq_ref[...], preferred_element_type=F32)
    for hd in range(MLA_HEADS):
        g = _rope(q[:, LANES * hd:LANES * (hd + 1)], tab_ref[0], tab_ref[1], MLA_ROPE // 4)
        q_ref[0, hd] = g.astype(BF16)

    pkv = proj(P_KV, P_GK)
    k_rope = _rope(pkv[:, LANES:], tab_ref[2], tab_ref[3], MLA_ROPE // 4)
    ckv = _rms(pkv[:, :LANES], kvn_ref[...], MLA_KV_RANK)
    kv = jnp.dot(ckv.astype(BF16), wukv_ref[...], preferred_element_type=F32)
    for hd in range(MLA_HEADS):
        k_ref[0, hd] = (kv[:, LANES * hd:LANES * (hd + 1)] + k_rope).astype(BF16)
    for j in range(MLA_HEADS // 2):
        lo = LANES * (MLA_HEADS + j)
        v_ref[0, j] = kv[:, lo:lo + LANES].T.astype(BF16)

    pg = proj(P_GK, P_GQ)
    for g in range(GQA_KV_HEADS):
        kk = _rms(pg[:, LANES * g:LANES * (g + 1)], gkn_ref[...], GQA_HEAD_DIM)
        k_ref[0, MLA_HEADS + g] = _rope(kk, tab_ref[6], tab_ref[7], GQA_HEAD_DIM // 4).astype(BF16)
    v_ref[0, MLA_HEADS // 2] = pg[:, 2 * LANES:].T.astype(BF16)

    pq = proj(P_GQ, P_CONV)
    for s in range(GQA_HEADS):
        qq = _rms(pq[:, LANES * s:LANES * (s + 1)], gqn_ref[...], GQA_HEAD_DIM)
        q_ref[0, MLA_HEADS + s] = _rope(qq, tab_ref[4], tab_ref[5], GQA_HEAD_DIM // 4).astype(BF16)

    pc = proj(P_CONV, P_GATE)
    glu_ref[0] = pc[:, :CONV_CH] * jax.nn.sigmoid(pc[:, CONV_CH:])
    gt = proj(P_GATE, P_COLS)
    sg_ref[0] = (gt * jax.nn.sigmoid(gt)).astype(BF16)


def _in_proj(xc, mod_l, n_lat, ctx_row, nw, tab, win, wuq, wukv, qn, kvn, gqn, gkn):
    batch, tt, _ = xc.shape
    nt = tt // TS

    def const(shape):
        return pl.BlockSpec(shape, lambda i, b: (0,) * len(shape))

    def heads(n):
        return pl.BlockSpec((1, n, TS, LANES), lambda i, b: (b, 0, i, 0))

    return pl.pallas_call(
        _in_proj_kernel,
        grid=(nt, batch),
        in_specs=[
            pl.BlockSpec((1, TS, D_MODEL), lambda i, b: (b, i, 0)),
            pl.BlockSpec((1, 1, 3 * D_MODEL), lambda i, b: (jnp.where(i < n_lat, b, ctx_row), 0, 0)),
            const((1, D_MODEL)),
            pl.BlockSpec((8, TS, LANES), lambda i, b: (0, i, 0)),
            const((D_MODEL, P_COLS)), const((MLA_Q_RANK, 768)), const((MLA_KV_RANK, 1152)),
            const((1, MLA_Q_RANK)), const((1, MLA_KV_RANK)), const((1, LANES)), const((1, LANES)),
        ],
        out_specs=[heads(N_QH), heads(N_KH),
                   pl.BlockSpec((1, N_VH, LANES, TS), lambda i, b: (b, 0, 0, i)),
                   pl.BlockSpec((1, TS, CONV_CH), lambda i, b: (b, i, 0)),
                   pl.BlockSpec((1, TS, MIX_WIDTH), lambda i, b: (b, i, 0))],
        out_shape=[jax.ShapeDtypeStruct((batch, N_QH, tt, LANES), BF16),
                   jax.ShapeDtypeStruct((batch, N_KH, tt, LANES), BF16),
                   jax.ShapeDtypeStruct((batch, N_VH, LANES, tt), BF16),
                   jax.ShapeDtypeStruct((batch, tt, CONV_CH), F32),
                   jax.ShapeDtypeStruct((batch, tt, MIX_WIDTH), BF16)],
        compiler_params=pltpu.CompilerParams(
            dimension_semantics=("arbitrary", "arbitrary"), vmem_limit_bytes=VMEM_LIMIT),
        name="in_proj",
    )(xc, mod_l, nw, tab, win, wuq, wukv, qn, kvn, gqn, gkn)


def _key_chunks(n_keys):
    chunks, lo = [], 0
    while lo < n_keys:
        size = min(KEY_CHUNK, n_keys - lo)
        chunks.append((lo, size))
        lo += size
    return chunks


def _attn_kernel(q_ref, k_ref, v_ref, o_ref, s0_ref, s1_ref, *, n_keys, tq):
    chunks = _key_chunks(n_keys)

    def fold8(x, op):
        return op(x.reshape(x.shape[0] // 8, 8, tq), axis=0)

    def key_index(h):
        return jnp.where(h < MLA_HEADS, h, MLA_HEADS + h % 2)

    def value_index(h):
        return jnp.minimum(h // 2, MLA_HEADS // 2)

    def stage(h_sm, m_sm, s_sm, h_qk, s_qk):
        if h_qk is not None:
            q = q_ref[0, h_qk]
            ki = key_index(h_qk)
            m8 = jnp.full((8, tq), -jnp.inf, F32)
        if h_sm is not None:
            vi = value_index(h_sm)
            l8 = jnp.zeros((8, tq), F32)
            o = jnp.zeros((LANES, tq), F32)
        for lo, size in chunks:
            if h_qk is not None:
                st = lax.dot_general(k_ref[0, ki, lo:lo + size, :], q, (((1,), (1,)), ((), ())),
                                     preferred_element_type=F32)
                s_qk[lo:lo + size, :] = st
                m8 = jnp.maximum(m8, fold8(st, jnp.max))
            if h_sm is not None:
                pt = jnp.exp2(s_sm[lo:lo + size, :] - m_sm)
                l8 = l8 + fold8(pt, jnp.sum)
                o = o + jnp.dot(v_ref[0, vi, :, lo:lo + size], pt.astype(BF16),
                                preferred_element_type=F32)
        out = o / jnp.sum(l8, axis=0, keepdims=True) if h_sm is not None else None
        m_new = jnp.max(m8, axis=0, keepdims=True) if h_qk is not None else None
        return out, m_new

    row = lax.broadcasted_iota(jnp.int32, (LANES, tq), 0)

    def pair(j, m_a, last):
        oa, m_b = stage(2 * j, m_a, s0_ref, 2 * j + 1, s1_ref)
        ob, m_next = stage(2 * j + 1, m_b, s1_ref, None if last else 2 * j + 2, s0_ref)
        o_ref[0, j] = jnp.where(row < 64, oa, ob).T.astype(BF16)
        return m_next

    _, m0 = stage(None, None, None, 0, s0_ref)
    m_last = lax.fori_loop(0, N_PAIRS - 1, lambda j, m: pair(j, m, False), m0)
    pair(N_PAIRS - 1, m_last, True)


def _attention(q_all, k_all, v_all, tq, q_tile0, n_q_tiles, key_block, n_keys):
    batch, _, tt, _ = q_all.shape
    return pl.pallas_call(
        functools.partial(_attn_kernel, n_keys=n_keys, tq=tq),
        grid=(batch, n_q_tiles),
        in_specs=[
            pl.BlockSpec((1, N_QH, tq, LANES), lambda b, i: (b, 0, q_tile0 + i, 0)),
            pl.BlockSpec((1, N_KH, n_keys, LANES), lambda b, i: (b, 0, key_block, 0)),
            pl.BlockSpec((1, N_VH, LANES, n_keys), lambda b, i: (b, 0, 0, key_block)),
        ],
        out_specs=pl.BlockSpec((1, N_PAIRS, tq, LANES), lambda b, i: (b, 0, i, 0)),
        out_shape=jax.ShapeDtypeStruct((batch, N_PAIRS, n_q_tiles * tq, LANES), BF16),
        scratch_shapes=[pltpu.VMEM((n_keys, tq), F32), pltpu.VMEM((n_keys, tq), F32)],
        compiler_params=pltpu.CompilerParams(
            dimension_semantics=("arbitrary", "arbitrary"), vmem_limit_bytes=VMEM_LIMIT),
        name="attention_%d" % n_keys,
    )(q_all, k_all, v_all)


def _out_proj_kernel(o_ref, glu_ref, prev_ref, next_ref, sg_ref, x_ref, mod_ref,
                     dww_ref, dwb_ref, lnw_ref, lnb_ref, pww_ref, pwb_ref, wout_ref, fnw_ref,
                     out_ref, ext_ref, sh_ref, *, n_lat, final):
    i = pl.program_id(0)
    has_prev = jnp.logical_and(i != 0, i != n_lat)
    has_next = jnp.logical_and(i != n_lat - 1, i != pl.num_programs(0) - 1)
    ext_ref[0:HALO_ROWS, :] = jnp.where(has_prev, prev_ref[0], 0.0)
    ext_ref[HALO_ROWS:HALO_ROWS + TS, :] = glu_ref[0]
    ext_ref[HALO_ROWS + TS:, :] = jnp.where(has_next, next_ref[0], 0.0)

    for r in range(8):
        sh_ref[r] = ext_ref[r:r + CONV_SPAN, :]
    acc = jnp.zeros((TS, CONV_CH), F32) + dwb_ref[...]
    for k in range(CONV_K):
        lo = HALO_ROWS - CONV_K // 2 + k
        acc = acc + sh_ref[lo % 8, lo - lo % 8:lo - lo % 8 + TS, :] * dww_ref[k:k + 1, :]
    mu = jnp.mean(acc, axis=-1, keepdims=True)
    cen = acc - mu
    var = jnp.mean(cen * cen, axis=-1, keepdims=True)
    z = cen * lax.rsqrt(var + EPS) * lnw_ref[...] + lnb_ref[...]
    z = z * jax.nn.sigmoid(z)
    o_conv = jnp.dot(z.astype(BF16), pww_ref[...], preferred_element_type=F32) + pwb_ref[...]

    mix = jnp.concatenate([o_ref[0, j].astype(F32) for j in range(N_PAIRS)] + [o_conv], axis=1)
    y = (mix * sg_ref[0].astype(F32)).astype(BF16)
    r = jnp.dot(y, wout_ref[...], preferred_element_type=F32)
    x_new = x_ref[0] + mod_ref[0, :, 2 * D_MODEL:] * r
    if final:
        x_new = _rms(x_new, fnw_ref[...], D_MODEL)
    out_ref[0] = x_new


def _out_proj(o_all, glu, sg, xc, mod_l, n_lat, ctx_row, n_tiles, dww, dwb, lnw, lnb, pww, pwb, wout,
              fnw, final):
    batch, tt, _ = xc.shape
    hb = TS // HALO_ROWS
    last_halo = tt // HALO_ROWS - 1

    def const(shape):
        return pl.BlockSpec(shape, lambda i, b: (0,) * len(shape))

    tile = lambda w: pl.BlockSpec((1, TS, w), lambda i, b: (b, i, 0))
    return pl.pallas_call(
        functools.partial(_out_proj_kernel, n_lat=n_lat, final=final),
        grid=(n_tiles, batch),
        in_specs=[
            pl.BlockSpec((1, N_PAIRS, TS, LANES), lambda i, b: (b, 0, i, 0)),
            tile(CONV_CH),
            pl.BlockSpec((1, HALO_ROWS, CONV_CH), lambda i, b: (b, jnp.maximum(i * hb - 1, 0), 0)),
            pl.BlockSpec((1, HALO_ROWS, CONV_CH),
                         lambda i, b: (b, jnp.minimum((i + 1) * hb, last_halo), 0)),
            tile(MIX_WIDTH), tile(D_MODEL),
            pl.BlockSpec((1, 1, 3 * D_MODEL), lambda i, b: (jnp.where(i < n_lat, b, ctx_row), 0, 0)),
            const((CONV_K, CONV_CH)), const((1, CONV_CH)), const((1, CONV_CH)), const((1, CONV_CH)),
            const((CONV_CH, CONV_CH)), const((1, CONV_CH)), const((MIX_WIDTH, D_MODEL)),
            const((1, D_MODEL)),
        ],
        out_specs=tile(D_MODEL),
        out_shape=jax.ShapeDtypeStruct((batch, n_tiles * TS, D_MODEL), F32),
        scratch_shapes=[pltpu.VMEM((TS + 2 * HALO_ROWS, CONV_CH), F32),
                        pltpu.VMEM((8, CONV_SPAN, CONV_CH), F32)],
        compiler_params=pltpu.CompilerParams(
            dimension_semantics=("arbitrary", "arbitrary"), vmem_limit_bytes=VMEM_LIMIT),
        name="out_proj_final" if final else "out_proj",
    )(o_all, glu, glu, glu, sg, xc, mod_l, dww, dwb, lnw, lnb, pww, pwb, wout, fnw)


def kernel(x, c, ctx, c_ctx, norm_w, w_mod, b_mod, w_in, mla_q_norm, mla_w_uq, mla_kv_norm,
           mla_w_ukv, gqa_q_norm, gqa_k_norm, conv_dw_w, conv_dw_b, conv_ln_w, conv_ln_b,
           conv_pw_w, conv_pw_b, w_out, final_norm_w):
    batch, seq, _ = x.shape
    ctx_len = ctx.shape[1]
    depth = w_in.shape[0]
    assert seq % TS == 0 and seq % TQ == 0 and ctx_len == TS and seq % GRID_W == 0
    n_lat = seq // TS
    tt = seq + ctx_len
    mod_rows = 16
    assert batch < mod_rows

    c_all = jnp.concatenate([c, c_ctx[None, :], jnp.zeros((mod_rows - batch - 1, D_MODEL), F32)], 0)
    mod = _modulation(c_all, w_mod, b_mod)
    tab = _rope_tables(seq, ctx_len)
    xc = jnp.concatenate([x, ctx], axis=1)
    row = lambda v: v.reshape(1, -1)
    dup = lambda v: jnp.concatenate([v, v]).reshape(1, LANES)

    for l in range(depth):
        final = l == depth - 1
        mod_l = mod[l].reshape(mod_rows, 1, 3 * D_MODEL)
        q_all, k_all, v_all, glu, sg = _in_proj(
            xc, mod_l, n_lat, batch, row(norm_w[l]), tab, _layout_w_in(w_in[l]),
            _layout_w_uq(mla_w_uq[l]), _layout_w_ukv(mla_w_ukv[l]), row(mla_q_norm[l]),
            row(mla_kv_norm[l]), dup(gqa_q_norm[l]), dup(gqa_k_norm[l]))
        o_lat = _attention(q_all, k_all, v_all, TQ, 0, seq // TQ, 0, tt)
        if final:
            o_all, n_tiles = o_lat, n_lat
        else:
            o_ctx = _attention(q_all, k_all, v_all, ctx_len, n_lat, 1, n_lat, ctx_len)
            o_all, n_tiles = jnp.concatenate([o_lat, o_ctx], axis=2), n_lat + 1
        xc = _out_proj(o_all, glu, sg, xc, mod_l, n_lat, batch, n_tiles,
                       conv_dw_w[l], row(conv_dw_b[l]), row(conv_ln_w[l]), row(conv_ln_b[l]),
                       conv_pw_w[l].astype(BF16), row(conv_pw_b[l]),
                       _mix_order(w_out[l]).astype(BF16), row(final_norm_w), final)
    return xc
```

```python
import functools
import math

import jax
import jax.numpy as jnp
from jax import lax
from jax.experimental import pallas as pl
from jax.experimental.pallas import tpu as pltpu

F32 = jnp.float32
BF16 = jnp.bfloat16

LANES = 128
HALO_ROWS = 16
VMEM_LIMIT = 56 * 1024 * 1024

D_MODEL = 1024
GRID_W = 64
ROPE_THETA = 10000.0
EPS = 1e-6
LOG2E = math.log2(math.e)

MLA_HEADS = 6
MLA_Q_RANK = 256
MLA_KV_RANK = 128
MLA_NOPE = 64
MLA_ROPE = 32
GQA_HEADS = 6
GQA_KV_HEADS = 2
GQA_HEAD_DIM = 64
CONV_CH = 256
CONV_K = 31
MIX_WIDTH = 1024

OFF_MLA_Q = 0
OFF_KV = 256
OFF_KR = OFF_KV + MLA_KV_RANK
OFF_GQA_K = OFF_KR + MLA_ROPE
OFF_GQA_V = OFF_GQA_K + 128
OFF_GQA_Q = OFF_GQA_V + 128
OFF_CONV = OFF_GQA_Q + 384
OFF_GATE = OFF_CONV + 2 * CONV_CH

P_CQ = 0
P_KV = 256
P_GK = 512
P_GQ = 768
P_CONV = 1152
P_GATE = 1664
P_COLS = 2688

GQA_SLOTS = (0, 3, 1, 4, 2, 5)

TS = 256
N_QH = 12
N_KH = 8
N_VH = 4
N_PAIRS = 6
TQ = 512
KEY_CHUNK = 512
CONV_SPAN = TS + 24


def _pad_cols(w, width):
    return jnp.pad(w, ((0, 0), (0, width - w.shape[1])))


def _layout_w_in(w_in):
    gate = w_in[:, OFF_GATE:]
    cols = [w_in[:, :OFF_KR], jnp.zeros((D_MODEL, 64), F32), w_in[:, OFF_KR:OFF_GQA_K],
            jnp.zeros((D_MODEL, 32), F32), w_in[:, OFF_GQA_K:OFF_GQA_Q]]
    cols += [w_in[:, OFF_GQA_Q + 64 * h:OFF_GQA_Q + 64 * (h + 1)] for h in GQA_SLOTS]
    cols += [w_in[:, OFF_CONV:OFF_GATE], gate[:, :384]]
    cols += [gate[:, 384 + 64 * h:384 + 64 * (h + 1)] for h in GQA_SLOTS]
    cols.append(gate[:, 768:])
    out = jnp.concatenate(cols, axis=1)
    assert out.shape == (D_MODEL, P_COLS)
    return out.astype(BF16)


def _mix_order(rows):
    gqa = [rows[384 + 64 * h:384 + 64 * (h + 1)] for h in GQA_SLOTS]
    return jnp.concatenate([rows[:384]] + gqa + [rows[768:]], axis=0)


def _layout_w_uq(w_uq):
    cols = [_pad_cols(w_uq[:, 96 * h:96 * (h + 1)], LANES) for h in range(MLA_HEADS)]
    return jnp.concatenate(cols, axis=1).astype(BF16)


def _layout_w_ukv(w_ukv):
    k = [_pad_cols(w_ukv[:, 128 * h:128 * h + 64], LANES) for h in range(MLA_HEADS)]
    v = [w_ukv[:, 128 * h + 64:128 * (h + 1)] for h in range(MLA_HEADS)]
    return jnp.concatenate(k + v, axis=1).astype(BF16)


def _rope_tables(seq, ctx_len):
    t = jnp.arange(seq, dtype=jnp.int32)
    row = (t // GRID_W).astype(F32)[:, None]
    col = (t % GRID_W).astype(F32)[:, None]

    def seg_tables(width):
        r = width // 2
        half = r // 2
        freqs = ROPE_THETA ** (-jnp.arange(half, dtype=F32) / half)
        cs, ss = [], []
        for pos in (row, col):
            ang = pos * freqs[None, :]
            c, s = jnp.cos(ang), jnp.sin(ang)
            cs += [c, c]
            ss += [-s, s]
        return jnp.concatenate(cs, axis=1), jnp.concatenate(ss, axis=1)

    def full(c, s, lo):
        w = c.shape[1]
        cf = jnp.concatenate([jnp.ones((seq, lo), F32), c, jnp.ones((seq, LANES - lo - w), F32)], 1)
        sf = jnp.concatenate([jnp.zeros((seq, lo), F32), s, jnp.zeros((seq, LANES - lo - w), F32)], 1)
        cf = jnp.concatenate([cf, jnp.ones((ctx_len, LANES), F32)], 0)
        sf = jnp.concatenate([sf, jnp.zeros((ctx_len, LANES), F32)], 0)
        return cf, sf

    mc, ms = full(*seg_tables(MLA_ROPE), MLA_NOPE)
    gc, gs = seg_tables(GQA_HEAD_DIM)
    gc, gs = full(jnp.tile(gc, (1, 2)), jnp.tile(gs, (1, 2)), 0)
    q_mla = (MLA_NOPE + MLA_ROPE) ** -0.5 * LOG2E
    q_gqa = GQA_HEAD_DIM ** -0.5 * LOG2E
    return jnp.stack([mc * q_mla, ms * q_mla, mc, ms, gc * q_gqa, gs * q_gqa, gc, gs])


def _mod_kernel(c_ref, w_ref, b_ref, o_ref):
    a = c_ref[...]
    a = a * jax.nn.sigmoid(a)
    o_ref[0] = jnp.dot(a, w_ref[0], preferred_element_type=F32,
                       precision=lax.Precision.HIGHEST) + b_ref[0]


def _modulation(c_all, w_mod, b_mod):
    depth = w_mod.shape[0]
    rows = c_all.shape[0]
    tn = 1024
    return pl.pallas_call(
        _mod_kernel,
        grid=(depth, 3 * D_MODEL // tn),
        in_specs=[pl.BlockSpec((rows, D_MODEL), lambda l, n: (0, 0)),
                  pl.BlockSpec((1, D_MODEL, tn), lambda l, n: (l, 0, n)),
                  pl.BlockSpec((1, 1, tn), lambda l, n: (l, 0, n))],
        out_specs=pl.BlockSpec((1, rows, tn), lambda l, n: (l, 0, n)),
        out_shape=jax.ShapeDtypeStruct((depth, rows, 3 * D_MODEL), F32),
        compiler_params=pltpu.CompilerParams(
            dimension_semantics=("arbitrary", "arbitrary"), vmem_limit_bytes=VMEM_LIMIT),
        name="adaln_mod",
    )(c_all, w_mod, b_mod.reshape(depth, 1, 3 * D_MODEL))


def _rms(x, gain, n):
    ss = jnp.sum(x * x, axis=-1, keepdims=True) * (1.0 / n)
    return x * lax.rsqrt(ss + EPS) * gain


def _rope(x, cos, sin, half):
    lane = lax.broadcasted_iota(jnp.int32, x.shape, 1)
    first = (lane % (2 * half)) < half
    partner = jnp.where(first, pltpu.roll(x, LANES - half, 1), pltpu.roll(x, half, 1))
    return x * cos + partner * sin


def _rms_halves(x, gain):
    low = lax.broadcasted_iota(jnp.int32, x.shape, 1) < 64
    x2 = x * x
    s_lo = jnp.sum(jnp.where(low, x2, 0.0), axis=-1, keepdims=True)
    s_hi = jnp.sum(jnp.where(low, 0.0, x2), axis=-1, keepdims=True)
    ss = jnp.where(low, s_lo, s_hi) * (1.0 / 64)
    return x * lax.rsqrt(ss + EPS) * gain


def _split_halves(x):
    low = lax.broadcasted_iota(jnp.int32, x.shape, 1) < 64
    return jnp.where(low, x, 0.0), jnp.where(low, 0.0, x)


def _in_proj_kernel(*refs, n_lat, split):
    if split:
        x_ref, ctx_ref = refs[:2]
        refs = refs[2:]
        x = jnp.where(pl.program_id(0) < n_lat, x_ref[0], ctx_ref[0])
    else:
        x = refs[0][0]
        refs = refs[1:]
    (mod_ref, nw_ref, tab_ref, win_ref, wuq_ref, wukv_ref, qn_ref, kvn_ref, gqn_ref, gkn_ref,
     q_ref, k_ref, v_ref, glu_ref, sg_ref) = refs
    shift = mod_ref[0, :, 0:D_MODEL]
    scale = mod_ref[0, :, D_MODEL:2 * D_MODEL]
    h = _rms(x, nw_ref[...], D_MODEL) * (1.0 + scale) + shift
    hb = h.astype(BF16)

    def proj(lo, hi):
        return jnp.dot(hb, win_ref[:, lo:hi], preferred_element_type=F32)

    pcq = proj(P_CQ, P_KV)
    pkv = proj(P_KV, P_GK)
    pq = proj(P_GQ, P_CONV)
    cq = _rms(pcq, qn_ref[...], MLA_Q_RANK)
    q = jnp.dot(cq.astype(BF16), wuq_ref[...], preferred_element_type=F32)
    ckv = _rms(pkv[:, :LANES], kvn_ref[...], MLA_KV_RANK)
    kv = jnp.dot(ckv.astype(BF16), wukv_ref[...], preferred_element_type=F32)
    pg = proj(P_GK, P_GQ)

    for j in range(GQA_HEADS // 2):
        qq = _rms_halves(pq[:, LANES * j:LANES * (j + 1)], gqn_ref[...])
        qa, qb = _split_halves(_rope(qq, tab_ref[4], tab_ref[5], GQA_HEAD_DIM // 4))
        q_ref[0, MLA_HEADS + 2 * j] = qa.astype(BF16)
        q_ref[0, MLA_HEADS + 2 * j + 1] = qb.astype(BF16)

    for hd in range(MLA_HEADS):
        g = _rope(q[:, LANES * hd:LANES * (hd + 1)], tab_ref[0], tab_ref[1], MLA_ROPE // 4)
        q_ref[0, hd] = g.astype(BF16)
    pc = proj(P_CONV, P_GATE)

    k_rope = _rope(pkv[:, LANES:], tab_ref[2], tab_ref[3], MLA_ROPE // 4)
    for hd in range(MLA_HEADS):
        k_ref[0, hd] = (kv[:, LANES * hd:LANES * (hd + 1)] + k_rope).astype(BF16)
    for j in range(MLA_HEADS // 2):
        lo = LANES * (MLA_HEADS + j)
        v_ref[0, j] = kv[:, lo:lo + LANES].T.astype(BF16)

    kk = _rms_halves(pg[:, :LANES], gkn_ref[...])
    ka, kb = _split_halves(_rope(kk, tab_ref[6], tab_ref[7], GQA_HEAD_DIM // 4))
    k_ref[0, MLA_HEADS] = ka.astype(BF16)
    k_ref[0, MLA_HEADS + 1] = kb.astype(BF16)
    v_ref[0, MLA_HEADS // 2] = pg[:, LANES:].T.astype(BF16)

    gt = proj(P_GATE, P_COLS)
    glu_ref[0] = pc[:, :CONV_CH] * jax.nn.sigmoid(pc[:, CONV_CH:])
    sg_ref[0] = (gt * jax.nn.sigmoid(gt)).astype(BF16)


def _in_proj(xs, mod_l, n_lat, ctx_row, nw, tab, win, wuq, wukv, qn, kvn, gqn, gkn):
    split = len(xs) == 2
    batch = xs[0].shape[0]
    nt = n_lat + 1
    tt = nt * TS

    def const(shape):
        return pl.BlockSpec(shape, lambda i, b: (0,) * len(shape))

    def heads(n):
        return pl.BlockSpec((1, n, TS, LANES), lambda i, b: (b, 0, i, 0))

    if split:
        x_specs = [pl.BlockSpec((1, TS, D_MODEL), lambda i, b: (b, jnp.minimum(i, n_lat - 1), 0)),
                   pl.BlockSpec((1, TS, D_MODEL), lambda i, b: (b, 0, 0))]
    else:
        x_specs = [pl.BlockSpec((1, TS, D_MODEL), lambda i, b: (b, i, 0))]
    return pl.pallas_call(
        functools.partial(_in_proj_kernel, n_lat=n_lat, split=split),
        grid=(nt, batch),
        in_specs=x_specs + [
            pl.BlockSpec((1, 1, 3 * D_MODEL), lambda i, b: (jnp.where(i < n_lat, b, ctx_row), 0, 0)),
            const((1, D_MODEL)),
            pl.BlockSpec((8, TS, LANES), lambda i, b: (0, i, 0)),
            const((D_MODEL, P_COLS)), const((MLA_Q_RANK, 768)), const((MLA_KV_RANK, 1152)),
            const((1, MLA_Q_RANK)), const((1, MLA_KV_RANK)), const((1, LANES)), const((1, LANES)),
        ],
        out_specs=[heads(N_QH), heads(N_KH),
                   pl.BlockSpec((1, N_VH, LANES, TS), lambda i, b: (b, 0, 0, i)),
                   pl.BlockSpec((1, TS, CONV_CH), lambda i, b: (b, i, 0)),
                   pl.BlockSpec((1, TS, MIX_WIDTH), lambda i, b: (b, i, 0))],
        out_shape=[jax.ShapeDtypeStruct((batch, N_QH, tt, LANES), BF16),
                   jax.ShapeDtypeStruct((batch, N_KH, tt, LANES), BF16),
                   jax.ShapeDtypeStruct((batch, N_VH, LANES, tt), BF16),
                   jax.ShapeDtypeStruct((batch, tt, CONV_CH), F32),
                   jax.ShapeDtypeStruct((batch, tt, MIX_WIDTH), BF16)],
        compiler_params=pltpu.CompilerParams(
            dimension_semantics=("arbitrary", "arbitrary"), vmem_limit_bytes=VMEM_LIMIT),
        name="in_proj",
    )(*xs, mod_l, nw, tab, win, wuq, wukv, qn, kvn, gqn, gkn)


def _key_chunks(n_keys):
    chunks, lo = [], 0
    while lo < n_keys:
        size = min(KEY_CHUNK, n_keys - lo)
        chunks.append((lo, size))
        lo += size
    return chunks


def _attn_kernel(q_ref, k_ref, v_ref, o_ref, s0_ref, s1_ref, *, n_keys, tq):
    chunks = _key_chunks(n_keys)

    def fold8(x, op):
        return op(x.reshape(x.shape[0] // 8, 8, tq), axis=0)

    def key_index(h):
        return jnp.where(h < MLA_HEADS, h, MLA_HEADS + h % 2)

    def value_index(h):
        return jnp.minimum(h // 2, MLA_HEADS // 2)

    def stage(h_sm, m_sm, s_sm, h_qk, s_qk):
        if h_qk is not None:
            q = q_ref[0, h_qk]
            ki = key_index(h_qk)
            m8 = jnp.full((8, tq), -jnp.inf, F32)
        if h_sm is not None:
            vi = value_index(h_sm)
            l8 = jnp.zeros((8, tq), F32)
            o = jnp.zeros((LANES, tq), F32)
        for lo, size in chunks:
            if h_qk is not None:
                st = lax.dot_general(k_ref[0, ki, lo:lo + size, :], q, (((1,), (1,)), ((), ())),
                                     preferred_element_type=F32)
                s_qk[lo:lo + size, :] = st
                m8 = jnp.maximum(m8, fold8(st, jnp.max))
            if h_sm is not None:
                pt = jnp.exp2(s_sm[lo:lo + size, :] - m_sm)
                l8 = l8 + fold8(pt, jnp.sum)
                o = o + jnp.dot(v_ref[0, vi, :, lo:lo + size], pt.astype(BF16),
                                preferred_element_type=F32)
        out = o / jnp.sum(l8, axis=0, keepdims=True) if h_sm is not None else None
        m_new = jnp.max(m8, axis=0, keepdims=True) if h_qk is not None else None
        return out, m_new

    row = lax.broadcasted_iota(jnp.int32, (LANES, tq), 0)

    def pair(j, m_a, last):
        oa, m_b = stage(2 * j, m_a, s0_ref, 2 * j + 1, s1_ref)
        ob, m_next = stage(2 * j + 1, m_b, s1_ref, None if last else 2 * j + 2, s0_ref)
        o_ref[0, j] = jnp.where(row < 64, oa, ob).T.astype(BF16)
        return m_next

    _, m0 = stage(None, None, None, 0, s0_ref)
    m_last = lax.fori_loop(0, N_PAIRS - 1, lambda j, m: pair(j, m, False), m0)
    pair(N_PAIRS - 1, m_last, True)


def _attention(q_all, k_all, v_all, tq, q_tile0, n_q_tiles, key_block, n_keys):
    batch, _, tt, _ = q_all.shape
    return pl.pallas_call(
        functools.partial(_attn_kernel, n_keys=n_keys, tq=tq),
        grid=(batch, n_q_tiles),
        in_specs=[
            pl.BlockSpec((1, N_QH, tq, LANES), lambda b, i: (b, 0, q_tile0 + i, 0)),
            pl.BlockSpec((1, N_KH, n_keys, LANES), lambda b, i: (b, 0, key_block, 0)),
            pl.BlockSpec((1, N_VH, LANES, n_keys), lambda b, i: (b, 0, 0, key_block)),
        ],
        out_specs=pl.BlockSpec((1, N_PAIRS, tq, LANES), lambda b, i: (b, 0, i, 0)),
        out_shape=jax.ShapeDtypeStruct((batch, N_PAIRS, n_q_tiles * tq, LANES), BF16),
        scratch_shapes=[pltpu.VMEM((n_keys, tq), F32), pltpu.VMEM((n_keys, tq), F32)],
        compiler_params=pltpu.CompilerParams(
            dimension_semantics=("arbitrary", "arbitrary"), vmem_limit_bytes=VMEM_LIMIT),
        name="attention_%d" % n_keys,
    )(q_all, k_all, v_all)


def _out_proj_kernel(*refs, n_lat, final, split):
    i = pl.program_id(0)
    if split:
        o_ref, octx_ref, glu_ref, prev_ref, next_ref, sg_ref, x_ref, xctx_ref = refs[:8]
        refs = refs[8:]
        lat = i < n_lat
        x = jnp.where(lat, x_ref[0], xctx_ref[0])
        o_pairs = [jnp.where(lat, o_ref[0, j], octx_ref[0, j]) for j in range(N_PAIRS)]
    else:
        o_ref, glu_ref, prev_ref, next_ref, sg_ref, x_ref = refs[:6]
        refs = refs[6:]
        x = x_ref[0]
        o_pairs = [o_ref[0, j] for j in range(N_PAIRS)]
    (mod_ref, dww_ref, dwb_ref, lnw_ref, lnb_ref, pww_ref, pwb_ref, wout_ref, fnw_ref,
     out_ref, ext_ref, sh_ref) = refs
    has_prev = jnp.logical_and(i != 0, i != n_lat)
    has_next = jnp.logical_and(i != n_lat - 1, i != pl.num_programs(0) - 1)
    ext_ref[0:HALO_ROWS, :] = jnp.where(has_prev, prev_ref[0], 0.0)
    ext_ref[HALO_ROWS:HALO_ROWS + TS, :] = glu_ref[0]
    ext_ref[HALO_ROWS + TS:, :] = jnp.where(has_next, next_ref[0], 0.0)

    for ph in range(8):
        sh_ref[ph] = ext_ref[ph:ph + CONV_SPAN, :]
    acc = jnp.zeros((TS, CONV_CH), F32) + dwb_ref[...]
    for k in range(CONV_K):
        lo = HALO_ROWS - CONV_K // 2 + k
        acc = acc + sh_ref[lo % 8, lo - lo % 8:lo - lo % 8 + TS, :] * dww_ref[k:k + 1, :]
    mu = jnp.mean(acc, axis=-1, keepdims=True)
    cen = acc - mu
    var = jnp.mean(cen * cen, axis=-1, keepdims=True)
    z = cen * lax.rsqrt(var + EPS) * lnw_ref[...] + lnb_ref[...]
    z = z * jax.nn.sigmoid(z)
    o_conv = jnp.dot(z.astype(BF16), pww_ref[...], preferred_element_type=F32) + pwb_ref[...]

    mix = jnp.concatenate([o.astype(F32) for o in o_pairs] + [o_conv], axis=1)
    y = (mix * sg_ref[0].astype(F32)).astype(BF16)
    r = jnp.dot(y, wout_ref[...], preferred_element_type=F32)
    x_new = x + mod_ref[0, :, 2 * D_MODEL:] * r
    if final:
        x_new = _rms(x_new, fnw_ref[...], D_MODEL)
    out_ref[0] = x_new


def _out_proj(os, glu, sg, xs, mod_l, n_lat, ctx_row, n_tiles, dww, dwb, lnw, lnb, pww, pwb, wout,
              fnw, final):
    split = len(xs) == 2
    batch = xs[0].shape[0]
    hb = TS // HALO_ROWS
    last_halo = glu.shape[1] // HALO_ROWS - 1

    def const(shape):
        return pl.BlockSpec(shape, lambda i, b: (0,) * len(shape))

    tile = lambda w: pl.BlockSpec((1, TS, w), lambda i, b: (b, i, 0))
    if split:
        lat = lambda i: jnp.minimum(i, n_lat - 1)
        o_specs = [pl.BlockSpec((1, N_PAIRS, TS, LANES), lambda i, b: (b, 0, lat(i), 0)),
                   pl.BlockSpec((1, N_PAIRS, TS, LANES), lambda i, b: (b, 0, 0, 0))]
        x_specs = [pl.BlockSpec((1, TS, D_MODEL), lambda i, b: (b, lat(i), 0)),
                   pl.BlockSpec((1, TS, D_MODEL), lambda i, b: (b, 0, 0))]
    else:
        o_specs = [pl.BlockSpec((1, N_PAIRS, TS, LANES), lambda i, b: (b, 0, i, 0))]
        x_specs = [tile(D_MODEL)]
    return pl.pallas_call(
        functools.partial(_out_proj_kernel, n_lat=n_lat, final=final, split=split),
        grid=(n_tiles, batch),
        in_specs=o_specs + [
            tile(CONV_CH),
            pl.BlockSpec((1, HALO_ROWS, CONV_CH), lambda i, b: (b, jnp.maximum(i * hb - 1, 0), 0)),
            pl.BlockSpec((1, HALO_ROWS, CONV_CH),
                         lambda i, b: (b, jnp.minimum((i + 1) * hb, last_halo), 0)),
            tile(MIX_WIDTH)] + x_specs + [
            pl.BlockSpec((1, 1, 3 * D_MODEL), lambda i, b: (jnp.where(i < n_lat, b, ctx_row), 0, 0)),
            const((CONV_K, CONV_CH)), const((1, CONV_CH)), const((1, CONV_CH)), const((1, CONV_CH)),
            const((CONV_CH, CONV_CH)), const((1, CONV_CH)), const((MIX_WIDTH, D_MODEL)),
            const((1, D_MODEL)),
        ],
        out_specs=tile(D_MODEL),
        out_shape=jax.ShapeDtypeStruct((batch, n_tiles * TS, D_MODEL), F32),
        scratch_shapes=[pltpu.VMEM((TS + 2 * HALO_ROWS, CONV_CH), F32),
                        pltpu.VMEM((8, CONV_SPAN, CONV_CH), F32)],
        compiler_params=pltpu.CompilerParams(
            dimension_semantics=("arbitrary", "arbitrary"), vmem_limit_bytes=VMEM_LIMIT),
        name="out_proj_final" if final else "out_proj",
    )(*os, glu, glu, glu, sg, *xs, mod_l, dww, dwb, lnw, lnb, pww, pwb, wout, fnw)


def kernel(x, c, ctx, c_ctx, norm_w, w_mod, b_mod, w_in, mla_q_norm, mla_w_uq, mla_kv_norm,
           mla_w_ukv, gqa_q_norm, gqa_k_norm, conv_dw_w, conv_dw_b, conv_ln_w, conv_ln_b,
           conv_pw_w, conv_pw_b, w_out, final_norm_w):
    batch, seq, _ = x.shape
    ctx_len = ctx.shape[1]
    depth = w_in.shape[0]
    assert seq % TS == 0 and seq % TQ == 0 and ctx_len == TS and seq % GRID_W == 0
    n_lat = seq // TS
    tt = seq + ctx_len
    mod_rows = 16
    assert batch < mod_rows

    c_all = jnp.concatenate([c, c_ctx[None, :], jnp.zeros((mod_rows - batch - 1, D_MODEL), F32)], 0)
    mod = _modulation(c_all, w_mod, b_mod)
    tab = _rope_tables(seq, ctx_len)
    row = lambda v: v.reshape(1, -1)
    dup = lambda v: jnp.concatenate([v, v]).reshape(1, LANES)

    xs = (x, ctx)
    for l in range(depth):
        final = l == depth - 1
        mod_l = mod[l].reshape(mod_rows, 1, 3 * D_MODEL)
        q_all, k_all, v_all, glu, sg = _in_proj(
            xs, mod_l, n_lat, batch, row(norm_w[l]), tab, _layout_w_in(w_in[l]),
            _layout_w_uq(mla_w_uq[l]), _layout_w_ukv(mla_w_ukv[l]), row(mla_q_norm[l]),
            row(mla_kv_norm[l]), dup(gqa_q_norm[l]), dup(gqa_k_norm[l]))
        o_lat = _attention(q_all, k_all, v_all, TQ, 0, seq // TQ, 0, tt)
        if final:
            os, n_tiles = (o_lat,), n_lat
        else:
            o_ctx = _attention(q_all, k_all, v_all, ctx_len, n_lat, 1, n_lat, ctx_len)
            os, n_tiles = (o_lat, o_ctx), n_lat + 1
        if not final and len(xs) == 1:
            xs = (xs[0][:, :seq], xs[0][:, seq:])
        xc = _out_proj(os, glu, sg, xs, mod_l, n_lat, batch, n_tiles,
                       conv_dw_w[l], row(conv_dw_b[l]), row(conv_ln_w[l]), row(conv_ln_b[l]),
                       conv_pw_w[l].astype(BF16), row(conv_pw_b[l]),
                       _mix_order(w_out[l]).astype(BF16), row(final_norm_w), final)
        xs = (xc,)
    return xc
```

```python
import functools
import math

import jax
import jax.numpy as jnp
from jax import lax
from jax.experimental import pallas as pl
from jax.experimental.pallas import tpu as pltpu

F32 = jnp.float32
BF16 = jnp.bfloat16

LANES = 128
HALO_ROWS = 16
VMEM_LIMIT = 56 * 1024 * 1024

D_MODEL = 1024
GRID_W = 64
ROPE_THETA = 10000.0
EPS = 1e-6
LOG2E = math.log2(math.e)

MLA_HEADS = 6
MLA_Q_RANK = 256
MLA_KV_RANK = 128
MLA_NOPE = 64
MLA_ROPE = 32
GQA_HEADS = 6
GQA_KV_HEADS = 2
GQA_HEAD_DIM = 64
CONV_CH = 256
CONV_K = 31
MIX_WIDTH = 1024

OFF_MLA_Q = 0
OFF_KV = 256
OFF_KR = OFF_KV + MLA_KV_RANK
OFF_GQA_K = OFF_KR + MLA_ROPE
OFF_GQA_V = OFF_GQA_K + 128
OFF_GQA_Q = OFF_GQA_V + 128
OFF_CONV = OFF_GQA_Q + 384
OFF_GATE = OFF_CONV + 2 * CONV_CH

P_CQ = 0
P_KV = 256
P_GK = 512
P_GQ = 768
P_CONV = 1152
P_GATE = 1664
P_COLS = 2688

GQA_SLOTS = (0, 3, 1, 4, 2, 5)

TS = 256
N_QH = 12
N_KH = 8
N_VH = 4
N_PAIRS = 6
TQ = 512
KEY_CHUNK = 512
CONV_PAD = 24
LAT_CHAINS = 2


def _pad_cols(w, width):
    return jnp.pad(w, ((0, 0), (0, width - w.shape[1])))


def _layout_w_in(w_in):
    gate = w_in[:, OFF_GATE:]
    cols = [w_in[:, :OFF_KR], jnp.zeros((D_MODEL, 64), F32), w_in[:, OFF_KR:OFF_GQA_K],
            jnp.zeros((D_MODEL, 32), F32), w_in[:, OFF_GQA_K:OFF_GQA_Q]]
    cols += [w_in[:, OFF_GQA_Q + 64 * h:OFF_GQA_Q + 64 * (h + 1)] for h in GQA_SLOTS]
    cols += [w_in[:, OFF_CONV:OFF_GATE], gate[:, :384]]
    cols += [gate[:, 384 + 64 * h:384 + 64 * (h + 1)] for h in GQA_SLOTS]
    cols.append(gate[:, 768:])
    out = jnp.concatenate(cols, axis=1)
    assert out.shape == (D_MODEL, P_COLS)
    return out.astype(BF16)


def _mix_order(rows):
    gqa = [rows[384 + 64 * h:384 + 64 * (h + 1)] for h in GQA_SLOTS]
    return jnp.concatenate([rows[:384]] + gqa + [rows[768:]], axis=0)


def _layout_w_uq(w_uq):
    cols = [_pad_cols(w_uq[:, 96 * h:96 * (h + 1)], LANES) for h in range(MLA_HEADS)]
    return jnp.concatenate(cols, axis=1).astype(BF16)


def _layout_w_ukv(w_ukv):
    k = [_pad_cols(w_ukv[:, 128 * h:128 * h + 64], LANES) for h in range(MLA_HEADS)]
    v = [w_ukv[:, 128 * h + 64:128 * (h + 1)] for h in range(MLA_HEADS)]
    return jnp.concatenate(k + v, axis=1).astype(BF16)


Q_SCALE_MLA = (MLA_NOPE + MLA_ROPE) ** -0.5 * LOG2E
Q_SCALE_GQA = GQA_HEAD_DIM ** -0.5 * LOG2E


def _rope_tables(seq):
    t = jnp.arange(seq, dtype=jnp.int32)
    row = (t // GRID_W).astype(F32)[:, None]
    col = (t % GRID_W).astype(F32)[:, None]

    def seg_tables(width):
        r = width // 2
        half = r // 2
        freqs = ROPE_THETA ** (-jnp.arange(half, dtype=F32) / half)
        cs, ss = [], []
        for pos in (row, col):
            ang = pos * freqs[None, :]
            c, s = jnp.cos(ang), jnp.sin(ang)
            cs += [c, c]
            ss += [-s, s]
        return jnp.concatenate(cs, axis=1), jnp.concatenate(ss, axis=1)

    def full(c, s, lo):
        w = c.shape[1]
        cf = jnp.concatenate([jnp.ones((seq, lo), F32), c, jnp.ones((seq, LANES - lo - w), F32)], 1)
        sf = jnp.concatenate([jnp.zeros((seq, lo), F32), s, jnp.zeros((seq, LANES - lo - w), F32)], 1)
        return cf, sf

    mc, ms = full(*seg_tables(MLA_ROPE), MLA_NOPE)
    gc, gs = seg_tables(GQA_HEAD_DIM)
    gc, gs = full(jnp.tile(gc, (1, 2)), jnp.tile(gs, (1, 2)), 0)
    return jnp.stack([mc * Q_SCALE_MLA, ms * Q_SCALE_MLA, mc, ms,
                      gc * Q_SCALE_GQA, gs * Q_SCALE_GQA, gc, gs])


def _identity_tables(rows):
    one = jnp.ones((rows, LANES), F32)
    zero = jnp.zeros((rows, LANES), F32)
    return jnp.stack([one * Q_SCALE_MLA, zero, one, zero, one * Q_SCALE_GQA, zero, one, zero])


def _mod_kernel(c_ref, w_ref, b_ref, o_ref):
    a = c_ref[...]
    a = a * jax.nn.sigmoid(a)
    o_ref[0] = jnp.dot(a, w_ref[0], preferred_element_type=F32,
                       precision=lax.Precision.HIGHEST) + b_ref[0]


def _modulation(c_all, w_mod, b_mod):
    depth = w_mod.shape[0]
    rows = c_all.shape[0]
    tn = 1024
    return pl.pallas_call(
        _mod_kernel,
        grid=(depth, 3 * D_MODEL // tn),
        in_specs=[pl.BlockSpec((rows, D_MODEL), lambda l, n: (0, 0)),
                  pl.BlockSpec((1, D_MODEL, tn), lambda l, n: (l, 0, n)),
                  pl.BlockSpec((1, 1, tn), lambda l, n: (l, 0, n))],
        out_specs=pl.BlockSpec((1, rows, tn), lambda l, n: (l, 0, n)),
        out_shape=jax.ShapeDtypeStruct((depth, rows, 3 * D_MODEL), F32),
        compiler_params=pltpu.CompilerParams(
            dimension_semantics=("arbitrary", "arbitrary"), vmem_limit_bytes=VMEM_LIMIT),
        name="adaln_mod",
    )(c_all, w_mod, b_mod.reshape(depth, 1, 3 * D_MODEL))


def _rms(x, gain, n):
    ss = jnp.sum(x * x, axis=-1, keepdims=True) * (1.0 / n)
    return x * lax.rsqrt(ss + EPS) * gain


def _rope(x, cos, sin, half):
    lane = lax.broadcasted_iota(jnp.int32, x.shape, 1)
    first = (lane % (2 * half)) < half
    partner = jnp.where(first, pltpu.roll(x, LANES - half, 1), pltpu.roll(x, half, 1))
    return x * cos + partner * sin


def _rms_halves(x, gain):
    low = lax.broadcasted_iota(jnp.int32, x.shape, 1) < 64
    x2 = x * x
    s_lo = jnp.sum(jnp.where(low, x2, 0.0), axis=-1, keepdims=True)
    s_hi = jnp.sum(jnp.where(low, 0.0, x2), axis=-1, keepdims=True)
    ss = jnp.where(low, s_lo, s_hi) * (1.0 / 64)
    return x * lax.rsqrt(ss + EPS) * gain


def _split_halves(x):
    low = lax.broadcasted_iota(jnp.int32, x.shape, 1) < 64
    return jnp.where(low, x, 0.0), jnp.where(low, 0.0, x)


def _in_proj_kernel(x_ref, mod_ref, nw_ref, tab_ref, win_ref, wuq_ref, wukv_ref,
                    qn_ref, kvn_ref, gqn_ref, gkn_ref,
                    q_ref, k_ref, v_ref, glu_ref, sg_ref, *, chains):
    shift = mod_ref[0, :, 0:D_MODEL]
    scale = mod_ref[0, :, D_MODEL:2 * D_MODEL]

    def chain(rows):
        h = _rms(x_ref[0, rows, :], nw_ref[...], D_MODEL) * (1.0 + scale) + shift
        hb = h.astype(BF16)

        def proj(lo, hi):
            return jnp.dot(hb, win_ref[:, lo:hi], preferred_element_type=F32)

        def tab(t):
            return tab_ref[t, rows, :]

        pcq = proj(P_CQ, P_KV)
        pkv = proj(P_KV, P_GK)
        pq = proj(P_GQ, P_CONV)
        cq = _rms(pcq, qn_ref[...], MLA_Q_RANK)
        q = jnp.dot(cq.astype(BF16), wuq_ref[...], preferred_element_type=F32)
        ckv = _rms(pkv[:, :LANES], kvn_ref[...], MLA_KV_RANK)
        kv = jnp.dot(ckv.astype(BF16), wukv_ref[...], preferred_element_type=F32)
        pg = proj(P_GK, P_GQ)

        for j in range(GQA_HEADS // 2):
            qq = _rms_halves(pq[:, LANES * j:LANES * (j + 1)], gqn_ref[...])
            qa, qb = _split_halves(_rope(qq, tab(4), tab(5), GQA_HEAD_DIM // 4))
            q_ref[0, MLA_HEADS + 2 * j, rows, :] = qa.astype(BF16)
            q_ref[0, MLA_HEADS + 2 * j + 1, rows, :] = qb.astype(BF16)

        for hd in range(MLA_HEADS):
            g = _rope(q[:, LANES * hd:LANES * (hd + 1)], tab(0), tab(1), MLA_ROPE // 4)
            q_ref[0, hd, rows, :] = g.astype(BF16)
        pc = proj(P_CONV, P_GATE)

        k_rope = _rope(pkv[:, LANES:], tab(2), tab(3), MLA_ROPE // 4)
        for hd in range(MLA_HEADS):
            k_ref[0, hd, rows, :] = (kv[:, LANES * hd:LANES * (hd + 1)] + k_rope).astype(BF16)
        for j in range(MLA_HEADS // 2):
            lo = LANES * (MLA_HEADS + j)
            v_ref[0, j, :, rows] = kv[:, lo:lo + LANES].T.astype(BF16)

        kk = _rms_halves(pg[:, :LANES], gkn_ref[...])
        ka, kb = _split_halves(_rope(kk, tab(6), tab(7), GQA_HEAD_DIM // 4))
        k_ref[0, MLA_HEADS, rows, :] = ka.astype(BF16)
        k_ref[0, MLA_HEADS + 1, rows, :] = kb.astype(BF16)
        v_ref[0, MLA_HEADS // 2, :, rows] = pg[:, LANES:].T.astype(BF16)

        gt = proj(P_GATE, P_COLS)
        glu_ref[0, rows, :] = pc[:, :CONV_CH] * jax.nn.sigmoid(pc[:, CONV_CH:])
        sg_ref[0, rows, :] = (gt * jax.nn.sigmoid(gt)).astype(BF16)

    for c in range(chains):
        chain(slice(c * TS, (c + 1) * TS))


def _in_proj(x, mod_l, mod_row, chains, nw, tab, win, wuq, wukv, qn, kvn, gqn, gkn):
    batch, rows, _ = x.shape
    tsz = chains * TS
    assert rows % tsz == 0

    def const(shape):
        return pl.BlockSpec(shape, lambda i, b: (0,) * len(shape))

    def heads(n):
        return pl.BlockSpec((1, n, tsz, LANES), lambda i, b: (b, 0, i, 0))

    return pl.pallas_call(
        functools.partial(_in_proj_kernel, chains=chains),
        grid=(rows // tsz, batch),
        in_specs=[
            pl.BlockSpec((1, tsz, D_MODEL), lambda i, b: (b, i, 0)),
            pl.BlockSpec((1, 1, 3 * D_MODEL),
                         lambda i, b: (b if mod_row is None else mod_row, 0, 0)),
            const((1, D_MODEL)),
            pl.BlockSpec((8, tsz, LANES), lambda i, b: (0, i, 0)),
            const((D_MODEL, P_COLS)), const((MLA_Q_RANK, 768)), const((MLA_KV_RANK, 1152)),
            const((1, MLA_Q_RANK)), const((1, MLA_KV_RANK)), const((1, LANES)), const((1, LANES)),
        ],
        out_specs=[heads(N_QH), heads(N_KH),
                   pl.BlockSpec((1, N_VH, LANES, tsz), lambda i, b: (b, 0, 0, i)),
                   pl.BlockSpec((1, tsz, CONV_CH), lambda i, b: (b, i, 0)),
                   pl.BlockSpec((1, tsz, MIX_WIDTH), lambda i, b: (b, i, 0))],
        out_shape=[jax.ShapeDtypeStruct((batch, N_QH, rows, LANES), BF16),
                   jax.ShapeDtypeStruct((batch, N_KH, rows, LANES), BF16),
                   jax.ShapeDtypeStruct((batch, N_VH, LANES, rows), BF16),
                   jax.ShapeDtypeStruct((batch, rows, CONV_CH), F32),
                   jax.ShapeDtypeStruct((batch, rows, MIX_WIDTH), BF16)],
        compiler_params=pltpu.CompilerParams(
            dimension_semantics=("arbitrary", "arbitrary"), vmem_limit_bytes=VMEM_LIMIT),
        name="in_proj_%d" % chains,
    )(x, mod_l, nw, tab, win, wuq, wukv, qn, kvn, gqn, gkn)


def _key_chunks(key_counts):
    chunks, row = [], 0
    for src, n in enumerate(key_counts):
        lo = 0
        while lo < n:
            size = min(KEY_CHUNK, n - lo)
            chunks.append((src, lo, size, row))
            lo += size
            row += size
    return chunks


def _attn_kernel(q_ref, *refs, key_counts, tq):
    n_src = len(key_counts)
    k_refs, v_refs = refs[:n_src], refs[n_src:2 * n_src]
    o_ref, s0_ref, s1_ref = refs[2 * n_src:]
    chunks = _key_chunks(key_counts)

    def fold8(x, op):
        return op(x.reshape(x.shape[0] // 8, 8, tq), axis=0)

    def key_index(h):
        return jnp.where(h < MLA_HEADS, h, MLA_HEADS + h % 2)

    def value_index(h):
        return jnp.minimum(h // 2, MLA_HEADS // 2)

    def stage(h_sm, m_sm, s_sm, h_qk, s_qk):
        if h_qk is not None:
            q = q_ref[0, h_qk]
            ki = key_index(h_qk)
            m8 = jnp.full((8, tq), -jnp.inf, F32)
        if h_sm is not None:
            vi = value_index(h_sm)
            l8 = jnp.zeros((8, tq), F32)
            o = jnp.zeros((LANES, tq), F32)
        for src, lo, size, row in chunks:
            if h_qk is not None:
                st = lax.dot_general(k_refs[src][0, ki, lo:lo + size, :], q,
                                     (((1,), (1,)), ((), ())),
                                     preferred_element_type=F32)
                s_qk[row:row + size, :] = st
                m8 = jnp.maximum(m8, fold8(st, jnp.max))
            if h_sm is not None:
                pt = jnp.exp2(s_sm[row:row + size, :] - m_sm)
                l8 = l8 + fold8(pt, jnp.sum)
                o = o + jnp.dot(v_refs[src][0, vi, :, lo:lo + size], pt.astype(BF16),
                                preferred_element_type=F32)
        out = o / jnp.sum(l8, axis=0, keepdims=True) if h_sm is not None else None
        m_new = jnp.max(m8, axis=0, keepdims=True) if h_qk is not None else None
        return out, m_new

    row_id = lax.broadcasted_iota(jnp.int32, (LANES, tq), 0)

    def pair(j, m_a, last):
        oa, m_b = stage(2 * j, m_a, s0_ref, 2 * j + 1, s1_ref)
        ob, m_next = stage(2 * j + 1, m_b, s1_ref, None if last else 2 * j + 2, s0_ref)
        o_ref[0, j] = jnp.where(row_id < 64, oa, ob).T.astype(BF16)
        return m_next

    _, m0 = stage(None, None, None, 0, s0_ref)
    m_last = lax.fori_loop(0, N_PAIRS - 1, lambda j, m: pair(j, m, False), m0)
    pair(N_PAIRS - 1, m_last, True)


def _attention(q, ks, vs, tq):
    batch, _, rows, _ = q.shape
    key_counts = tuple(k.shape[2] for k in ks)
    n_keys = sum(key_counts)
    return pl.pallas_call(
        functools.partial(_attn_kernel, key_counts=key_counts, tq=tq),
        grid=(batch, rows // tq),
        in_specs=[pl.BlockSpec((1, N_QH, tq, LANES), lambda b, i: (b, 0, i, 0))]
        + [pl.BlockSpec((1, N_KH, n, LANES), lambda b, i: (b, 0, 0, 0)) for n in key_counts]
        + [pl.BlockSpec((1, N_VH, LANES, n), lambda b, i: (b, 0, 0, 0)) for n in key_counts],
        out_specs=pl.BlockSpec((1, N_PAIRS, tq, LANES), lambda b, i: (b, 0, i, 0)),
        out_shape=jax.ShapeDtypeStruct((batch, N_PAIRS, rows, LANES), BF16),
        scratch_shapes=[pltpu.VMEM((n_keys, tq), F32), pltpu.VMEM((n_keys, tq), F32)],
        compiler_params=pltpu.CompilerParams(
            dimension_semantics=("arbitrary", "arbitrary"), vmem_limit_bytes=VMEM_LIMIT),
        name="attention_%d" % n_keys,
    )(q, *ks, *vs)


def _out_proj_kernel(o_ref, glu_ref, prev_ref, next_ref, sg_ref, x_ref, mod_ref,
                     dww_ref, dwb_ref, lnw_ref, lnb_ref, pww_ref, pwb_ref, wout_ref, fnw_ref,
                     out_ref, ext_ref, sh_ref, *, chains, final):
    i = pl.program_id(0)
    rows_all = chains * TS
    ext_ref[0:HALO_ROWS, :] = jnp.where(i != 0, prev_ref[0], 0.0)
    ext_ref[HALO_ROWS:HALO_ROWS + rows_all, :] = glu_ref[0]
    ext_ref[HALO_ROWS + rows_all:, :] = jnp.where(i != pl.num_programs(0) - 1, next_ref[0], 0.0)

    for ph in range(8):
        sh_ref[ph] = ext_ref[ph:ph + rows_all + CONV_PAD, :]

    def chain(r0):
        rows = slice(r0, r0 + TS)
        acc = jnp.zeros((TS, CONV_CH), F32) + dwb_ref[...]
        for k in range(CONV_K):
            lo = HALO_ROWS - CONV_K // 2 + k
            base = r0 + lo - lo % 8
            acc = acc + sh_ref[lo % 8, base:base + TS, :] * dww_ref[k:k + 1, :]
        mu = jnp.mean(acc, axis=-1, keepdims=True)
        cen = acc - mu
        var = jnp.mean(cen * cen, axis=-1, keepdims=True)
        z = cen * lax.rsqrt(var + EPS) * lnw_ref[...] + lnb_ref[...]
        z = z * jax.nn.sigmoid(z)
        o_conv = jnp.dot(z.astype(BF16), pww_ref[...], preferred_element_type=F32) + pwb_ref[...]

        mix = jnp.concatenate([o_ref[0, j, rows, :].astype(F32) for j in range(N_PAIRS)] + [o_conv],
                              axis=1)
        y = (mix * sg_ref[0, rows, :].astype(F32)).astype(BF16)
        r = jnp.dot(y, wout_ref[...], preferred_element_type=F32)
        x_new = x_ref[0, rows, :] + mod_ref[0, :, 2 * D_MODEL:] * r
        if final:
            x_new = _rms(x_new, fnw_ref[...], D_MODEL)
        out_ref[0, rows, :] = x_new

    for c in range(chains):
        chain(c * TS)


def _out_proj(o, glu, sg, x, mod_l, mod_row, chains, dww, dwb, lnw, lnb, pww, pwb, wout, fnw, final):
    batch, rows, _ = x.shape
    tsz = chains * TS
    assert rows % tsz == 0
    hb = tsz // HALO_ROWS
    last_halo = rows // HALO_ROWS - 1

    def const(shape):
        return pl.BlockSpec(shape, lambda i, b: (0,) * len(shape))

    tile = lambda w: pl.BlockSpec((1, tsz, w), lambda i, b: (b, i, 0))
    return pl.pallas_call(
        functools.partial(_out_proj_kernel, chains=chains, final=final),
        grid=(rows // tsz, batch),
        in_specs=[
            pl.BlockSpec((1, N_PAIRS, tsz, LANES), lambda i, b: (b, 0, i, 0)),
            tile(CONV_CH),
            pl.BlockSpec((1, HALO_ROWS, CONV_CH), lambda i, b: (b, jnp.maximum(i * hb - 1, 0), 0)),
            pl.BlockSpec((1, HALO_ROWS, CONV_CH),
                         lambda i, b: (b, jnp.minimum((i + 1) * hb, last_halo), 0)),
            tile(MIX_WIDTH), tile(D_MODEL),
            pl.BlockSpec((1, 1, 3 * D_MODEL),
                         lambda i, b: (b if mod_row is None else mod_row, 0, 0)),
            const((CONV_K, CONV_CH)), const((1, CONV_CH)), const((1, CONV_CH)), const((1, CONV_CH)),
            const((CONV_CH, CONV_CH)), const((1, CONV_CH)), const((MIX_WIDTH, D_MODEL)),
            const((1, D_MODEL)),
        ],
        out_specs=tile(D_MODEL),
        out_shape=jax.ShapeDtypeStruct((batch, rows, D_MODEL), F32),
        scratch_shapes=[pltpu.VMEM((tsz + 2 * HALO_ROWS, CONV_CH), F32),
                        pltpu.VMEM((8, tsz + CONV_PAD, CONV_CH), F32)],
        compiler_params=pltpu.CompilerParams(
            dimension_semantics=("arbitrary", "arbitrary"), vmem_limit_bytes=VMEM_LIMIT),
        name="out_proj_%d%s" % (chains, "_final" if final else ""),
    )(o, glu, glu, glu, sg, x, mod_l, dww, dwb, lnw, lnb, pww, pwb, wout, fnw)


def kernel(x, c, ctx, c_ctx, norm_w, w_mod, b_mod, w_in, mla_q_norm, mla_w_uq, mla_kv_norm,
           mla_w_ukv, gqa_q_norm, gqa_k_norm, conv_dw_w, conv_dw_b, conv_ln_w, conv_ln_b,
           conv_pw_w, conv_pw_b, w_out, final_norm_w):
    batch, seq, _ = x.shape
    ctx_len = ctx.shape[1]
    depth = w_in.shape[0]
    assert seq % (LAT_CHAINS * TS) == 0 and seq % TQ == 0 and seq % GRID_W == 0
    assert ctx_len == TS
    mod_rows = 16
    assert batch < mod_rows

    c_all = jnp.concatenate([c, c_ctx[None, :], jnp.zeros((mod_rows - batch - 1, D_MODEL), F32)], 0)
    mod = _modulation(c_all, w_mod, b_mod)
    tab_lat = _rope_tables(seq)
    tab_ctx = _identity_tables(ctx_len)
    row = lambda v: v.reshape(1, -1)
    dup = lambda v: jnp.concatenate([v, v]).reshape(1, LANES)

    for l in range(depth):
        final = l == depth - 1
        mod_l = mod[l].reshape(mod_rows, 1, 3 * D_MODEL)
        weights = (_layout_w_in(w_in[l]), _layout_w_uq(mla_w_uq[l]), _layout_w_ukv(mla_w_ukv[l]),
                   row(mla_q_norm[l]), row(mla_kv_norm[l]), dup(gqa_q_norm[l]), dup(gqa_k_norm[l]))
        q_lat, k_lat, v_lat, glu_lat, sg_lat = _in_proj(
            x, mod_l, None, LAT_CHAINS, row(norm_w[l]), tab_lat, *weights)
        q_ctx, k_ctx, v_ctx, glu_ctx, sg_ctx = _in_proj(
            ctx, mod_l, batch, 1, row(norm_w[l]), tab_ctx, *weights)
        conv = (conv_dw_w[l], row(conv_dw_b[l]), row(conv_ln_w[l]), row(conv_ln_b[l]),
                conv_pw_w[l].astype(BF16), row(conv_pw_b[l]),
                _mix_order(w_out[l]).astype(BF16), row(final_norm_w))
        o_lat = _attention(q_lat, (k_lat, k_ctx), (v_lat, v_ctx), TQ)
        if not final:
            o_ctx = _attention(q_ctx, (k_ctx,), (v_ctx,), ctx_len)
            ctx = _out_proj(o_ctx, glu_ctx, sg_ctx, ctx, mod_l, batch, 1, *conv, False)
        x = _out_proj(o_lat, glu_lat, sg_lat, x, mod_l, None, LAT_CHAINS, *conv, final)
    return x
```

```python
import functools
import math

import jax
import jax.numpy as jnp
from jax import lax
from jax.experimental import pallas as pl
from jax.experimental.pallas import tpu as pltpu

F32 = jnp.float32
BF16 = jnp.bfloat16

LANES = 128
HALO_ROWS = 16
VMEM_LIMIT = 56 * 1024 * 1024

D_MODEL = 1024
GRID_W = 64
ROPE_THETA = 10000.0
EPS = 1e-6
LOG2E = math.log2(math.e)

MLA_HEADS = 6
MLA_Q_RANK = 256
MLA_KV_RANK = 128
MLA_NOPE = 64
MLA_ROPE = 32
GQA_HEADS = 6
GQA_KV_HEADS = 2
GQA_HEAD_DIM = 64
CONV_CH = 256
CONV_K = 31
MIX_WIDTH = 1024

OFF_MLA_Q = 0
OFF_KV = 256
OFF_KR = OFF_KV + MLA_KV_RANK
OFF_GQA_K = OFF_KR + MLA_ROPE
OFF_GQA_V = OFF_GQA_K + 128
OFF_GQA_Q = OFF_GQA_V + 128
OFF_CONV = OFF_GQA_Q + 384
OFF_GATE = OFF_CONV + 2 * CONV_CH

P_CQ = 0
P_KV = 256
P_GK = 512
P_GQ = 768
P_CONV = 1152
P_GATE = 1664
P_COLS = 2688

GQA_SLOTS = (0, 3, 1, 4, 2, 5)

TS = 256
N_QH = 12
N_KH = 8
N_VH = 4
N_PAIRS = 6
TQ = 512
KEY_CHUNK = 512
CONV_PAD = 24
LAT_CHAINS = 4


def _pad_cols(w, width):
    return jnp.pad(w, ((0, 0), (0, width - w.shape[1])))


def _layout_w_in(w_in):
    gate = w_in[:, OFF_GATE:]
    cols = [w_in[:, :OFF_KR], jnp.zeros((D_MODEL, 64), F32), w_in[:, OFF_KR:OFF_GQA_K],
            jnp.zeros((D_MODEL, 32), F32), w_in[:, OFF_GQA_K:OFF_GQA_Q]]
    cols += [w_in[:, OFF_GQA_Q + 64 * h:OFF_GQA_Q + 64 * (h + 1)] for h in GQA_SLOTS]
    cols += [w_in[:, OFF_CONV:OFF_GATE], gate[:, :384]]
    cols += [gate[:, 384 + 64 * h:384 + 64 * (h + 1)] for h in GQA_SLOTS]
    cols.append(gate[:, 768:])
    out = jnp.concatenate(cols, axis=1)
    assert out.shape == (D_MODEL, P_COLS)
    return out.astype(BF16)


def _mix_order(rows):
    gqa = [rows[384 + 64 * h:384 + 64 * (h + 1)] for h in GQA_SLOTS]
    return jnp.concatenate([rows[:384]] + gqa + [rows[768:]], axis=0)


def _layout_w_uq(w_uq):
    cols = [_pad_cols(w_uq[:, 96 * h:96 * (h + 1)], LANES) for h in range(MLA_HEADS)]
    return jnp.concatenate(cols, axis=1).astype(BF16)


def _layout_w_ukv(w_ukv):
    k = [_pad_cols(w_ukv[:, 128 * h:128 * h + 64], LANES) for h in range(MLA_HEADS)]
    v = [w_ukv[:, 128 * h + 64:128 * (h + 1)] for h in range(MLA_HEADS)]
    return jnp.concatenate(k + v, axis=1).astype(BF16)


Q_SCALE_MLA = (MLA_NOPE + MLA_ROPE) ** -0.5 * LOG2E
Q_SCALE_GQA = GQA_HEAD_DIM ** -0.5 * LOG2E


def _rope_tables(seq):
    t = jnp.arange(seq, dtype=jnp.int32)
    row = (t // GRID_W).astype(F32)[:, None]
    col = (t % GRID_W).astype(F32)[:, None]

    def seg_tables(width):
        r = width // 2
        half = r // 2
        freqs = ROPE_THETA ** (-jnp.arange(half, dtype=F32) / half)
        cs, ss = [], []
        for pos in (row, col):
            ang = pos * freqs[None, :]
            c, s = jnp.cos(ang), jnp.sin(ang)
            cs += [c, c]
            ss += [-s, s]
        return jnp.concatenate(cs, axis=1), jnp.concatenate(ss, axis=1)

    def full(c, s, lo):
        w = c.shape[1]
        cf = jnp.concatenate([jnp.ones((seq, lo), F32), c, jnp.ones((seq, LANES - lo - w), F32)], 1)
        sf = jnp.concatenate([jnp.zeros((seq, lo), F32), s, jnp.zeros((seq, LANES - lo - w), F32)], 1)
        return cf, sf

    mc, ms = full(*seg_tables(MLA_ROPE), MLA_NOPE)
    gc, gs = seg_tables(GQA_HEAD_DIM)
    gc, gs = full(jnp.tile(gc, (1, 2)), jnp.tile(gs, (1, 2)), 0)
    return jnp.stack([mc * Q_SCALE_MLA, ms * Q_SCALE_MLA, mc, ms,
                      gc * Q_SCALE_GQA, gs * Q_SCALE_GQA, gc, gs])


def _identity_tables(rows):
    one = jnp.ones((rows, LANES), F32)
    zero = jnp.zeros((rows, LANES), F32)
    return jnp.stack([one * Q_SCALE_MLA, zero, one, zero, one * Q_SCALE_GQA, zero, one, zero])


def _mod_kernel(c_ref, w_ref, b_ref, o_ref):
    a = c_ref[...]
    a = a * jax.nn.sigmoid(a)
    o_ref[0] = jnp.dot(a, w_ref[0], preferred_element_type=F32,
                       precision=lax.Precision.HIGHEST) + b_ref[0]


def _modulation(c_all, w_mod, b_mod):
    depth = w_mod.shape[0]
    rows = c_all.shape[0]
    tn = 1024
    return pl.pallas_call(
        _mod_kernel,
        grid=(depth, 3 * D_MODEL // tn),
        in_specs=[pl.BlockSpec((rows, D_MODEL), lambda l, n: (0, 0)),
                  pl.BlockSpec((1, D_MODEL, tn), lambda l, n: (l, 0, n)),
                  pl.BlockSpec((1, 1, tn), lambda l, n: (l, 0, n))],
        out_specs=pl.BlockSpec((1, rows, tn), lambda l, n: (l, 0, n)),
        out_shape=jax.ShapeDtypeStruct((depth, rows, 3 * D_MODEL), F32),
        compiler_params=pltpu.CompilerParams(
            dimension_semantics=("arbitrary", "arbitrary"), vmem_limit_bytes=VMEM_LIMIT),
        name="adaln_mod",
    )(c_all, w_mod, b_mod.reshape(depth, 1, 3 * D_MODEL))


def _rms(x, gain, n):
    ss = jnp.sum(x * x, axis=-1, keepdims=True) * (1.0 / n)
    return x * lax.rsqrt(ss + EPS) * gain


def _rope(x, cos, sin, half):
    lane = lax.broadcasted_iota(jnp.int32, x.shape, 1)
    first = (lane % (2 * half)) < half
    partner = jnp.where(first, pltpu.roll(x, LANES - half, 1), pltpu.roll(x, half, 1))
    return x * cos + partner * sin


def _rms_halves(x, gain):
    low = lax.broadcasted_iota(jnp.int32, x.shape, 1) < 64
    x2 = x * x
    s_lo = jnp.sum(jnp.where(low, x2, 0.0), axis=-1, keepdims=True)
    s_hi = jnp.sum(jnp.where(low, 0.0, x2), axis=-1, keepdims=True)
    ss = jnp.where(low, s_lo, s_hi) * (1.0 / 64)
    return x * lax.rsqrt(ss + EPS) * gain


def _split_halves(x):
    low = lax.broadcasted_iota(jnp.int32, x.shape, 1) < 64
    return jnp.where(low, x, 0.0), jnp.where(low, 0.0, x)


def _in_proj_kernel(x_ref, mod_ref, nw_ref, tab_ref, win_ref, wuq_ref, wukv_ref,
                    qn_ref, kvn_ref, gqn_ref, gkn_ref, *out_refs, chains, kv_only):
    if kv_only:
        k_ref, v_ref = out_refs
    else:
        q_ref, k_ref, v_ref, glu_ref, sg_ref = out_refs
    shift = mod_ref[0, :, 0:D_MODEL]
    scale = mod_ref[0, :, D_MODEL:2 * D_MODEL]
    rows = [slice(c * TS, (c + 1) * TS) for c in range(chains)]
    each = range(chains)

    def tab(t, c):
        return tab_ref[t, rows[c], :]

    hb, pcq, pkv, pq, pg = [], [], [], [], []
    for c in each:
        h = _rms(x_ref[0, rows[c], :], nw_ref[...], D_MODEL) * (1.0 + scale) + shift
        hb.append(h.astype(BF16))
        proj = lambda lo, hi, c=c: jnp.dot(hb[c], win_ref[:, lo:hi], preferred_element_type=F32)
        pkv.append(proj(P_KV, P_GK))
        pg.append(proj(P_GK, P_GQ))
        if not kv_only:
            pcq.append(proj(P_CQ, P_KV))
            pq.append(proj(P_GQ, P_CONV))

    def proj(c, lo, hi):
        return jnp.dot(hb[c], win_ref[:, lo:hi], preferred_element_type=F32)

    q, kv, pc = [], [], []
    for c in each:
        kv.append(jnp.dot(_rms(pkv[c][:, :LANES], kvn_ref[...], MLA_KV_RANK).astype(BF16),
                          wukv_ref[...], preferred_element_type=F32))
        if not kv_only:
            q.append(jnp.dot(_rms(pcq[c], qn_ref[...], MLA_Q_RANK).astype(BF16), wuq_ref[...],
                             preferred_element_type=F32))
            pc.append(proj(c, P_CONV, P_GATE))

    for c in each if not kv_only else ():
        for j in range(GQA_HEADS // 2):
            qq = _rms_halves(pq[c][:, LANES * j:LANES * (j + 1)], gqn_ref[...])
            qa, qb = _split_halves(_rope(qq, tab(4, c), tab(5, c), GQA_HEAD_DIM // 4))
            q_ref[0, MLA_HEADS + 2 * j, rows[c], :] = qa.astype(BF16)
            q_ref[0, MLA_HEADS + 2 * j + 1, rows[c], :] = qb.astype(BF16)
        for hd in range(MLA_HEADS):
            g = _rope(q[c][:, LANES * hd:LANES * (hd + 1)], tab(0, c), tab(1, c), MLA_ROPE // 4)
            q_ref[0, hd, rows[c], :] = g.astype(BF16)

    gt = [proj(c, P_GATE, P_COLS) for c in each] if not kv_only else None

    for c in each:
        k_rope = _rope(pkv[c][:, LANES:], tab(2, c), tab(3, c), MLA_ROPE // 4)
        for hd in range(MLA_HEADS):
            k_ref[0, hd, rows[c], :] = (kv[c][:, LANES * hd:LANES * (hd + 1)] + k_rope).astype(BF16)
        for j in range(MLA_HEADS // 2):
            lo = LANES * (MLA_HEADS + j)
            v_ref[0, j, :, rows[c]] = kv[c][:, lo:lo + LANES].T.astype(BF16)
        kk = _rms_halves(pg[c][:, :LANES], gkn_ref[...])
        ka, kb = _split_halves(_rope(kk, tab(6, c), tab(7, c), GQA_HEAD_DIM // 4))
        k_ref[0, MLA_HEADS, rows[c], :] = ka.astype(BF16)
        k_ref[0, MLA_HEADS + 1, rows[c], :] = kb.astype(BF16)
        v_ref[0, MLA_HEADS // 2, :, rows[c]] = pg[c][:, LANES:].T.astype(BF16)
        if not kv_only:
            glu_ref[0, rows[c], :] = pc[c][:, :CONV_CH] * jax.nn.sigmoid(pc[c][:, CONV_CH:])
            sg_ref[0, rows[c], :] = (gt[c] * jax.nn.sigmoid(gt[c])).astype(BF16)


def _in_proj(x, mod_l, mod_row, chains, kv_only, nw, tab, win, wuq, wukv, qn, kvn, gqn, gkn):
    batch, rows, _ = x.shape
    tsz = chains * TS
    assert rows % tsz == 0

    def const(shape):
        return pl.BlockSpec(shape, lambda i, b: (0,) * len(shape))

    def heads(n):
        return pl.BlockSpec((1, n, tsz, LANES), lambda i, b: (b, 0, i, 0))

    out_specs = [heads(N_QH), heads(N_KH),
                 pl.BlockSpec((1, N_VH, LANES, tsz), lambda i, b: (b, 0, 0, i)),
                 pl.BlockSpec((1, tsz, CONV_CH), lambda i, b: (b, i, 0)),
                 pl.BlockSpec((1, tsz, MIX_WIDTH), lambda i, b: (b, i, 0))]
    out_shape = [jax.ShapeDtypeStruct((batch, N_QH, rows, LANES), BF16),
                 jax.ShapeDtypeStruct((batch, N_KH, rows, LANES), BF16),
                 jax.ShapeDtypeStruct((batch, N_VH, LANES, rows), BF16),
                 jax.ShapeDtypeStruct((batch, rows, CONV_CH), F32),
                 jax.ShapeDtypeStruct((batch, rows, MIX_WIDTH), BF16)]
    return pl.pallas_call(
        functools.partial(_in_proj_kernel, chains=chains, kv_only=kv_only),
        grid=(rows // tsz, batch),
        in_specs=[
            pl.BlockSpec((1, tsz, D_MODEL), lambda i, b: (b, i, 0)),
            pl.BlockSpec((1, 1, 3 * D_MODEL),
                         lambda i, b: (b if mod_row is None else mod_row, 0, 0)),
            const((1, D_MODEL)),
            pl.BlockSpec((8, tsz, LANES), lambda i, b: (0, i, 0)),
            const((D_MODEL, P_COLS)), const((MLA_Q_RANK, 768)), const((MLA_KV_RANK, 1152)),
            const((1, MLA_Q_RANK)), const((1, MLA_KV_RANK)), const((1, LANES)), const((1, LANES)),
        ],
        out_specs=out_specs[1:3] if kv_only else out_specs,
        out_shape=out_shape[1:3] if kv_only else out_shape,
        compiler_params=pltpu.CompilerParams(
            dimension_semantics=("arbitrary", "arbitrary"), vmem_limit_bytes=VMEM_LIMIT),
        name="in_proj_%d%s" % (chains, "_kv" if kv_only else ""),
    )(x, mod_l, nw, tab, win, wuq, wukv, qn, kvn, gqn, gkn)


def _key_chunks(key_counts):
    chunks, row = [], 0
    for src, n in enumerate(key_counts):
        lo = 0
        while lo < n:
            size = min(KEY_CHUNK, n - lo)
            chunks.append((src, lo, size, row))
            lo += size
            row += size
    return chunks


def _attn_kernel(q_ref, *refs, key_counts, tq):
    n_src = len(key_counts)
    k_refs, v_refs = refs[:n_src], refs[n_src:2 * n_src]
    o_ref, s0_ref, s1_ref = refs[2 * n_src:]
    chunks = _key_chunks(key_counts)

    def fold8(x, op):
        return op(x.reshape(x.shape[0] // 8, 8, tq), axis=0)

    def key_index(h):
        return jnp.where(h < MLA_HEADS, h, MLA_HEADS + h % 2)

    def value_index(h):
        return jnp.minimum(h // 2, MLA_HEADS // 2)

    def stage(h_sm, m_sm, s_sm, h_qk, s_qk):
        if h_qk is not None:
            q = q_ref[0, h_qk]
            ki = key_index(h_qk)
            m8 = jnp.full((8, tq), -jnp.inf, F32)
        if h_sm is not None:
            vi = value_index(h_sm)
            l8 = jnp.zeros((8, tq), F32)
            o = jnp.zeros((LANES, tq), F32)
        for src, lo, size, row in chunks:
            if h_qk is not None:
                st = lax.dot_general(k_refs[src][0, ki, lo:lo + size, :], q,
                                     (((1,), (1,)), ((), ())),
                                     preferred_element_type=F32)
                s_qk[row:row + size, :] = st
                m8 = jnp.maximum(m8, fold8(st, jnp.max))
            if h_sm is not None:
                pt = jnp.exp2(s_sm[row:row + size, :] - m_sm)
                l8 = l8 + fold8(pt, jnp.sum)
                o = o + jnp.dot(v_refs[src][0, vi, :, lo:lo + size], pt.astype(BF16),
                                preferred_element_type=F32)
        out = o / jnp.sum(l8, axis=0, keepdims=True) if h_sm is not None else None
        m_new = jnp.max(m8, axis=0, keepdims=True) if h_qk is not None else None
        return out, m_new

    row_id = lax.broadcasted_iota(jnp.int32, (LANES, tq), 0)

    def pair(j, m_a, last):
        oa, m_b = stage(2 * j, m_a, s0_ref, 2 * j + 1, s1_ref)
        ob, m_next = stage(2 * j + 1, m_b, s1_ref, None if last else 2 * j + 2, s0_ref)
        o_ref[0, j] = jnp.where(row_id < 64, oa, ob).T.astype(BF16)
        return m_next

    _, m0 = stage(None, None, None, 0, s0_ref)
    m_last = lax.fori_loop(0, N_PAIRS - 1, lambda j, m: pair(j, m, False), m0)
    pair(N_PAIRS - 1, m_last, True)


def _attention(q, ks, vs, tq):
    batch, _, rows, _ = q.shape
    key_counts = tuple(k.shape[2] for k in ks)
    n_keys = sum(key_counts)
    return pl.pallas_call(
        functools.partial(_attn_kernel, key_counts=key_counts, tq=tq),
        grid=(batch, rows // tq),
        in_specs=[pl.BlockSpec((1, N_QH, tq, LANES), lambda b, i: (b, 0, i, 0))]
        + [pl.BlockSpec((1, N_KH, n, LANES), lambda b, i: (b, 0, 0, 0)) for n in key_counts]
        + [pl.BlockSpec((1, N_VH, LANES, n), lambda b, i: (b, 0, 0, 0)) for n in key_counts],
        out_specs=pl.BlockSpec((1, N_PAIRS, tq, LANES), lambda b, i: (b, 0, i, 0)),
        out_shape=jax.ShapeDtypeStruct((batch, N_PAIRS, rows, LANES), BF16),
        scratch_shapes=[pltpu.VMEM((n_keys, tq), F32), pltpu.VMEM((n_keys, tq), F32)],
        compiler_params=pltpu.CompilerParams(
            dimension_semantics=("arbitrary", "arbitrary"), vmem_limit_bytes=VMEM_LIMIT),
        name="attention_%d" % n_keys,
    )(q, *ks, *vs)


def _out_proj_kernel(o_ref, glu_ref, prev_ref, next_ref, sg_ref, x_ref, mod_ref,
                     dww_ref, dwb_ref, lnw_ref, lnb_ref, pww_ref, pwb_ref, wout_ref, fnw_ref,
                     out_ref, ext_ref, sh_ref, *, chains, final):
    i = pl.program_id(0)
    rows_all = chains * TS
    ext_ref[0:HALO_ROWS, :] = jnp.where(i != 0, prev_ref[0], 0.0)
    ext_ref[HALO_ROWS:HALO_ROWS + rows_all, :] = glu_ref[0]
    ext_ref[HALO_ROWS + rows_all:, :] = jnp.where(i != pl.num_programs(0) - 1, next_ref[0], 0.0)

    for ph in range(8):
        sh_ref[ph] = ext_ref[ph:ph + rows_all + CONV_PAD, :]

    def chain(r0):
        rows = slice(r0, r0 + TS)
        acc = jnp.zeros((TS, CONV_CH), F32) + dwb_ref[...]
        for k in range(CONV_K):
            lo = HALO_ROWS - CONV_K // 2 + k
            base = r0 + lo - lo % 8
            acc = acc + sh_ref[lo % 8, base:base + TS, :] * dww_ref[k:k + 1, :]
        mu = jnp.mean(acc, axis=-1, keepdims=True)
        cen = acc - mu
        var = jnp.mean(cen * cen, axis=-1, keepdims=True)
        z = cen * lax.rsqrt(var + EPS) * lnw_ref[...] + lnb_ref[...]
        z = z * jax.nn.sigmoid(z)
        o_conv = jnp.dot(z.astype(BF16), pww_ref[...], preferred_element_type=F32) + pwb_ref[...]

        mix = jnp.concatenate([o_ref[0, j, rows, :].astype(F32) for j in range(N_PAIRS)] + [o_conv],
                              axis=1)
        y = (mix * sg_ref[0, rows, :].astype(F32)).astype(BF16)
        r = jnp.dot(y, wout_ref[...], preferred_element_type=F32)
        x_new = x_ref[0, rows, :] + mod_ref[0, :, 2 * D_MODEL:] * r
        if final:
            x_new = _rms(x_new, fnw_ref[...], D_MODEL)
        out_ref[0, rows, :] = x_new

    for c in range(chains):
        chain(c * TS)


def _out_proj(o, glu, sg, x, mod_l, mod_row, chains, dww, dwb, lnw, lnb, pww, pwb, wout, fnw, final):
    batch, rows, _ = x.shape
    tsz = chains * TS
    assert rows % tsz == 0
    hb = tsz // HALO_ROWS
    last_halo = rows // HALO_ROWS - 1

    def const(shape):
        return pl.BlockSpec(shape, lambda i, b: (0,) * len(shape))

    tile = lambda w: pl.BlockSpec((1, tsz, w), lambda i, b: (b, i, 0))
    return pl.pallas_call(
        functools.partial(_out_proj_kernel, chains=chains, final=final),
        grid=(rows // tsz, batch),
        in_specs=[
            pl.BlockSpec((1, N_PAIRS, tsz, LANES), lambda i, b: (b, 0, i, 0)),
            tile(CONV_CH),
            pl.BlockSpec((1, HALO_ROWS, CONV_CH), lambda i, b: (b, jnp.maximum(i * hb - 1, 0), 0)),
            pl.BlockSpec((1, HALO_ROWS, CONV_CH),
                         lambda i, b: (b, jnp.minimum((i + 1) * hb, last_halo), 0)),
            tile(MIX_WIDTH), tile(D_MODEL),
            pl.BlockSpec((1, 1, 3 * D_MODEL),
                         lambda i, b: (b if mod_row is None else mod_row, 0, 0)),
            const((CONV_K, CONV_CH)), const((1, CONV_CH)), const((1, CONV_CH)), const((1, CONV_CH)),
            const((CONV_CH, CONV_CH)), const((1, CONV_CH)), const((MIX_WIDTH, D_MODEL)),
            const((1, D_MODEL)),
        ],
        out_specs=tile(D_MODEL),
        out_shape=jax.ShapeDtypeStruct((batch, rows, D_MODEL), F32),
        scratch_shapes=[pltpu.VMEM((tsz + 2 * HALO_ROWS, CONV_CH), F32),
                        pltpu.VMEM((8, tsz + CONV_PAD, CONV_CH), F32)],
        compiler_params=pltpu.CompilerParams(
            dimension_semantics=("arbitrary", "arbitrary"), vmem_limit_bytes=VMEM_LIMIT),
        name="out_proj_%d%s" % (chains, "_final" if final else ""),
    )(o, glu, glu, glu, sg, x, mod_l, dww, dwb, lnw, lnb, pww, pwb, wout, fnw)


def kernel(x, c, ctx, c_ctx, norm_w, w_mod, b_mod, w_in, mla_q_norm, mla_w_uq, mla_kv_norm,
           mla_w_ukv, gqa_q_norm, gqa_k_norm, conv_dw_w, conv_dw_b, conv_ln_w, conv_ln_b,
           conv_pw_w, conv_pw_b, w_out, final_norm_w):
    batch, seq, _ = x.shape
    ctx_len = ctx.shape[1]
    depth = w_in.shape[0]
    assert seq % (LAT_CHAINS * TS) == 0 and seq % TQ == 0 and seq % GRID_W == 0
    assert ctx_len == TS
    mod_rows = 16
    assert batch < mod_rows

    c_all = jnp.concatenate([c, c_ctx[None, :], jnp.zeros((mod_rows - batch - 1, D_MODEL), F32)], 0)
    mod = _modulation(c_all, w_mod, b_mod)
    tab_lat = _rope_tables(seq)
    tab_ctx = _identity_tables(ctx_len)
    row = lambda v: v.reshape(1, -1)
    dup = lambda v: jnp.concatenate([v, v]).reshape(1, LANES)

    for l in range(depth):
        final = l == depth - 1
        mod_l = mod[l].reshape(mod_rows, 1, 3 * D_MODEL)
        weights = (_layout_w_in(w_in[l]), _layout_w_uq(mla_w_uq[l]), _layout_w_ukv(mla_w_ukv[l]),
                   row(mla_q_norm[l]), row(mla_kv_norm[l]), dup(gqa_q_norm[l]), dup(gqa_k_norm[l]))
        q_lat, k_lat, v_lat, glu_lat, sg_lat = _in_proj(
            x, mod_l, None, LAT_CHAINS, False, row(norm_w[l]), tab_lat, *weights)
        if final:
            k_ctx, v_ctx = _in_proj(ctx, mod_l, batch, 1, True, row(norm_w[l]), tab_ctx, *weights)
        else:
            q_ctx, k_ctx, v_ctx, glu_ctx, sg_ctx = _in_proj(
                ctx, mod_l, batch, 1, False, row(norm_w[l]), tab_ctx, *weights)
        conv = (conv_dw_w[l], row(conv_dw_b[l]), row(conv_ln_w[l]), row(conv_ln_b[l]),
                conv_pw_w[l].astype(BF16), row(conv_pw_b[l]),
                _mix_order(w_out[l]).astype(BF16), row(final_norm_w))
        o_lat = _attention(q_lat, (k_lat, k_ctx), (v_lat, v_ctx), TQ)
        if not final:
            o_ctx = _attention(q_ctx, (k_ctx,), (v_ctx,), ctx_len)
            ctx = _out_proj(o_ctx, glu_ctx, sg_ctx, ctx, mod_l, batch, 1, *conv, False)
        x = _out_proj(o_lat, glu_lat, sg_lat, x, mod_l, None, LAT_CHAINS, *conv, final)
    return x
```

```python
import functools
import math

import jax
import jax.numpy as jnp
from jax import lax
from jax.experimental import pallas as pl
from jax.experimental.pallas import tpu as pltpu

F32 = jnp.float32
BF16 = jnp.bfloat16

LANES = 128
HALO_ROWS = 16
VMEM_LIMIT = 56 * 1024 * 1024

D_MODEL = 1024
GRID_W = 64
ROPE_THETA = 10000.0
EPS = 1e-6
LOG2E = math.log2(math.e)

MLA_HEADS = 6
MLA_Q_RANK = 256
MLA_KV_RANK = 128
MLA_NOPE = 64
MLA_ROPE = 32
GQA_HEADS = 6
GQA_KV_HEADS = 2
GQA_HEAD_DIM = 64
CONV_CH = 256
CONV_K = 31
MIX_WIDTH = 1024

OFF_MLA_Q = 0
OFF_KV = 256
OFF_KR = OFF_KV + MLA_KV_RANK
OFF_GQA_K = OFF_KR + MLA_ROPE
OFF_GQA_V = OFF_GQA_K + 128
OFF_GQA_Q = OFF_GQA_V + 128
OFF_CONV = OFF_GQA_Q + 384
OFF_GATE = OFF_CONV + 2 * CONV_CH

P_CQ = 0
P_KV = 256
P_GK = 512
P_GQ = 768
P_CONV = 1152
P_GATE = 1664
P_COLS = 2688

GQA_SLOTS = (0, 3, 1, 4, 2, 5)

TS = 256
N_QH = 12
N_KH = 8
N_VH = 4
N_PAIRS = 6
TQ = 512
KEY_CHUNK = 512
CONV_PAD = 24
LAT_CHAINS = 4


def _pad_cols(w, width):
    return jnp.pad(w, ((0, 0), (0, width - w.shape[1])))


def _layout_w_in(w_in):
    gate = w_in[:, OFF_GATE:]
    cols = [w_in[:, :OFF_KR], jnp.zeros((D_MODEL, 64), F32), w_in[:, OFF_KR:OFF_GQA_K],
            jnp.zeros((D_MODEL, 32), F32), w_in[:, OFF_GQA_K:OFF_GQA_Q]]
    cols += [w_in[:, OFF_GQA_Q + 64 * h:OFF_GQA_Q + 64 * (h + 1)] for h in GQA_SLOTS]
    cols += [w_in[:, OFF_CONV:OFF_GATE], gate[:, :384]]
    cols += [gate[:, 384 + 64 * h:384 + 64 * (h + 1)] for h in GQA_SLOTS]
    cols.append(gate[:, 768:])
    out = jnp.concatenate(cols, axis=1)
    assert out.shape == (D_MODEL, P_COLS)
    return out.astype(BF16)


def _mix_order(rows):
    gqa = [rows[384 + 64 * h:384 + 64 * (h + 1)] for h in GQA_SLOTS]
    return jnp.concatenate([rows[:384]] + gqa + [rows[768:]], axis=0)


def _layout_w_uq(w_uq):
    cols = [_pad_cols(w_uq[:, 96 * h:96 * (h + 1)], LANES) for h in range(MLA_HEADS)]
    return jnp.concatenate(cols, axis=1).astype(BF16)


def _layout_w_ukv(w_ukv):
    k = [_pad_cols(w_ukv[:, 128 * h:128 * h + 64], LANES) for h in range(MLA_HEADS)]
    v = [w_ukv[:, 128 * h + 64:128 * (h + 1)] for h in range(MLA_HEADS)]
    return jnp.concatenate(k + v, axis=1).astype(BF16)


Q_SCALE_MLA = (MLA_NOPE + MLA_ROPE) ** -0.5 * LOG2E
Q_SCALE_GQA = GQA_HEAD_DIM ** -0.5 * LOG2E


def _rope_tables(seq):
    t = jnp.arange(seq, dtype=jnp.int32)
    row = (t // GRID_W).astype(F32)[:, None]
    col = (t % GRID_W).astype(F32)[:, None]

    def seg_tables(width):
        r = width // 2
        half = r // 2
        freqs = ROPE_THETA ** (-jnp.arange(half, dtype=F32) / half)
        cs, ss = [], []
        for pos in (row, col):
            ang = pos * freqs[None, :]
            c, s = jnp.cos(ang), jnp.sin(ang)
            cs += [c, c]
            ss += [-s, s]
        return jnp.concatenate(cs, axis=1), jnp.concatenate(ss, axis=1)

    def full(c, s, lo):
        w = c.shape[1]
        cf = jnp.concatenate([jnp.ones((seq, lo), F32), c, jnp.ones((seq, LANES - lo - w), F32)], 1)
        sf = jnp.concatenate([jnp.zeros((seq, lo), F32), s, jnp.zeros((seq, LANES - lo - w), F32)], 1)
        return cf, sf

    mc, ms = full(*seg_tables(MLA_ROPE), MLA_NOPE)
    gc, gs = seg_tables(GQA_HEAD_DIM)
    gc, gs = full(jnp.tile(gc, (1, 2)), jnp.tile(gs, (1, 2)), 0)
    return jnp.stack([mc * Q_SCALE_MLA, ms * Q_SCALE_MLA, mc, ms,
                      gc * Q_SCALE_GQA, gs * Q_SCALE_GQA, gc, gs])


def _identity_tables(rows):
    one = jnp.ones((rows, LANES), F32)
    zero = jnp.zeros((rows, LANES), F32)
    return jnp.stack([one * Q_SCALE_MLA, zero, one, zero, one * Q_SCALE_GQA, zero, one, zero])


def _mod_kernel(c_ref, w_ref, b_ref, o_ref):
    a = c_ref[...]
    a = a * jax.nn.sigmoid(a)
    o_ref[0] = jnp.dot(a, w_ref[0], preferred_element_type=F32,
                       precision=lax.Precision.HIGHEST) + b_ref[0]


def _modulation(c_all, w_mod, b_mod):
    depth = w_mod.shape[0]
    rows = c_all.shape[0]
    tn = 1024
    return pl.pallas_call(
        _mod_kernel,
        grid=(depth, 3 * D_MODEL // tn),
        in_specs=[pl.BlockSpec((rows, D_MODEL), lambda l, n: (0, 0)),
                  pl.BlockSpec((1, D_MODEL, tn), lambda l, n: (l, 0, n)),
                  pl.BlockSpec((1, 1, tn), lambda l, n: (l, 0, n))],
        out_specs=pl.BlockSpec((1, rows, tn), lambda l, n: (l, 0, n)),
        out_shape=jax.ShapeDtypeStruct((depth, rows, 3 * D_MODEL), F32),
        compiler_params=pltpu.CompilerParams(
            dimension_semantics=("arbitrary", "arbitrary"), vmem_limit_bytes=VMEM_LIMIT),
        name="adaln_mod",
    )(c_all, w_mod, b_mod.reshape(depth, 1, 3 * D_MODEL))


def _rms(x, gain, n):
    ss = jnp.sum(x * x, axis=-1, keepdims=True) * (1.0 / n)
    return x * lax.rsqrt(ss + EPS) * gain


def _rope(x, cos, sin, half):
    lane = lax.broadcasted_iota(jnp.int32, x.shape, 1)
    first = (lane % (2 * half)) < half
    partner = jnp.where(first, pltpu.roll(x, LANES - half, 1), pltpu.roll(x, half, 1))
    return x * cos + partner * sin


def _rms_halves(x, gain):
    low = lax.broadcasted_iota(jnp.int32, x.shape, 1) < 64
    x2 = x * x
    s_lo = jnp.sum(jnp.where(low, x2, 0.0), axis=-1, keepdims=True)
    s_hi = jnp.sum(jnp.where(low, 0.0, x2), axis=-1, keepdims=True)
    ss = jnp.where(low, s_lo, s_hi) * (1.0 / 64)
    return x * lax.rsqrt(ss + EPS) * gain


def _split_halves(x):
    low = lax.broadcasted_iota(jnp.int32, x.shape, 1) < 64
    return jnp.where(low, x, 0.0), jnp.where(low, 0.0, x)


def _in_proj_kernel(x_ref, mod_ref, nw_ref, tab_ref, win_ref, wuq_ref, wukv_ref,
                    qn_ref, kvn_ref, gqn_ref, gkn_ref, *out_refs, chains, kv_only):
    if kv_only:
        k_ref, v_ref = out_refs
    else:
        q_ref, k_ref, v_ref, glu_ref, sg_ref = out_refs
    shift = mod_ref[0, :, 0:D_MODEL]
    scale = mod_ref[0, :, D_MODEL:2 * D_MODEL]
    rows = [slice(c * TS, (c + 1) * TS) for c in range(chains)]
    each = range(chains)

    def tab(t, c):
        return tab_ref[t, rows[c], :]

    hb, pcq, pkv, pq, pg = [], [], [], [], []
    for c in each:
        h = _rms(x_ref[0, rows[c], :], nw_ref[...], D_MODEL) * (1.0 + scale) + shift
        hb.append(h.astype(BF16))
        proj = lambda lo, hi, c=c: jnp.dot(hb[c], win_ref[:, lo:hi], preferred_element_type=F32)
        pkv.append(proj(P_KV, P_GK))
        pg.append(proj(P_GK, P_GQ))
        if not kv_only:
            pcq.append(proj(P_CQ, P_KV))
            pq.append(proj(P_GQ, P_CONV))

    def proj(c, lo, hi):
        return jnp.dot(hb[c], win_ref[:, lo:hi], preferred_element_type=F32)

    q, kv, pc = [], [], []
    for c in each:
        kv.append(jnp.dot(_rms(pkv[c][:, :LANES], kvn_ref[...], MLA_KV_RANK).astype(BF16),
                          wukv_ref[...], preferred_element_type=F32))
        if not kv_only:
            q.append(jnp.dot(_rms(pcq[c], qn_ref[...], MLA_Q_RANK).astype(BF16), wuq_ref[...],
                             preferred_element_type=F32))
            pc.append(proj(c, P_CONV, P_GATE))

    for c in each if not kv_only else ():
        for j in range(GQA_HEADS // 2):
            qq = _rms_halves(pq[c][:, LANES * j:LANES * (j + 1)], gqn_ref[...])
            qa, qb = _split_halves(_rope(qq, tab(4, c), tab(5, c), GQA_HEAD_DIM // 4))
            q_ref[0, MLA_HEADS + 2 * j, rows[c], :] = qa.astype(BF16)
            q_ref[0, MLA_HEADS + 2 * j + 1, rows[c], :] = qb.astype(BF16)
        for hd in range(MLA_HEADS):
            g = _rope(q[c][:, LANES * hd:LANES * (hd + 1)], tab(0, c), tab(1, c), MLA_ROPE // 4)
            q_ref[0, hd, rows[c], :] = g.astype(BF16)

    gt = [proj(c, P_GATE, P_COLS) for c in each] if not kv_only else None

    for c in each:
        k_rope = _rope(pkv[c][:, LANES:], tab(2, c), tab(3, c), MLA_ROPE // 4)
        for hd in range(MLA_HEADS):
            k_ref[0, hd, rows[c], :] = (kv[c][:, LANES * hd:LANES * (hd + 1)] + k_rope).astype(BF16)
        for j in range(MLA_HEADS // 2):
            lo = LANES * (MLA_HEADS + j)
            v_ref[0, j, :, rows[c]] = kv[c][:, lo:lo + LANES].T.astype(BF16)
        kk = _rms_halves(pg[c][:, :LANES], gkn_ref[...])
        ka, kb = _split_halves(_rope(kk, tab(6, c), tab(7, c), GQA_HEAD_DIM // 4))
        k_ref[0, MLA_HEADS, rows[c], :] = ka.astype(BF16)
        k_ref[0, MLA_HEADS + 1, rows[c], :] = kb.astype(BF16)
        v_ref[0, MLA_HEADS // 2, :, rows[c]] = pg[c][:, LANES:].T.astype(BF16)
        if not kv_only:
            glu_ref[0, rows[c], :] = pc[c][:, :CONV_CH] * jax.nn.sigmoid(pc[c][:, CONV_CH:])
            sg_ref[0, rows[c], :] = (gt[c] * jax.nn.sigmoid(gt[c])).astype(BF16)


def _in_proj(x, mod_l, mod_row, chains, kv_only, nw, tab, win, wuq, wukv, qn, kvn, gqn, gkn):
    batch, rows, _ = x.shape
    tsz = chains * TS
    assert rows % tsz == 0

    def const(shape):
        return pl.BlockSpec(shape, lambda i, b: (0,) * len(shape))

    def heads(n):
        return pl.BlockSpec((1, n, tsz, LANES), lambda i, b: (b, 0, i, 0))

    out_specs = [heads(N_QH), heads(N_KH),
                 pl.BlockSpec((1, N_VH, LANES, tsz), lambda i, b: (b, 0, 0, i)),
                 pl.BlockSpec((1, tsz, CONV_CH), lambda i, b: (b, i, 0)),
                 pl.BlockSpec((1, tsz, MIX_WIDTH), lambda i, b: (b, i, 0))]
    out_shape = [jax.ShapeDtypeStruct((batch, N_QH, rows, LANES), BF16),
                 jax.ShapeDtypeStruct((batch, N_KH, rows, LANES), BF16),
                 jax.ShapeDtypeStruct((batch, N_VH, LANES, rows), BF16),
                 jax.ShapeDtypeStruct((batch, rows, CONV_CH), F32),
                 jax.ShapeDtypeStruct((batch, rows, MIX_WIDTH), BF16)]
    return pl.pallas_call(
        functools.partial(_in_proj_kernel, chains=chains, kv_only=kv_only),
        grid=(rows // tsz, batch),
        in_specs=[
            pl.BlockSpec((1, tsz, D_MODEL), lambda i, b: (b, i, 0)),
            pl.BlockSpec((1, 1, 3 * D_MODEL),
                         lambda i, b: (b if mod_row is None else mod_row, 0, 0)),
            const((1, D_MODEL)),
            pl.BlockSpec((8, tsz, LANES), lambda i, b: (0, i, 0)),
            const((D_MODEL, P_COLS)), const((MLA_Q_RANK, 768)), const((MLA_KV_RANK, 1152)),
            const((1, MLA_Q_RANK)), const((1, MLA_KV_RANK)), const((1, LANES)), const((1, LANES)),
        ],
        out_specs=out_specs[1:3] if kv_only else out_specs,
        out_shape=out_shape[1:3] if kv_only else out_shape,
        compiler_params=pltpu.CompilerParams(
            dimension_semantics=("arbitrary", "arbitrary"), vmem_limit_bytes=VMEM_LIMIT),
        name="in_proj_%d%s" % (chains, "_kv" if kv_only else ""),
    )(x, mod_l, nw, tab, win, wuq, wukv, qn, kvn, gqn, gkn)


def _key_chunks(key_counts):
    chunks, row = [], 0
    for src, n in enumerate(key_counts):
        lo = 0
        while lo < n:
            size = min(KEY_CHUNK, n - lo)
            chunks.append((src, lo, size, row))
            lo += size
            row += size
    return chunks


def _attn_kernel(q_ref, qn_ref, *refs, key_counts, tq):
    n_src = len(key_counts)
    k_refs, v_refs = refs[:n_src], refs[n_src:2 * n_src]
    o_ref, s0_ref, s1_ref, m_ref = refs[2 * n_src:]
    chunks = _key_chunks(key_counts)

    def fold8(x, op):
        return op(x.reshape(x.shape[0] // 8, 8, tq), axis=0)

    def key_index(h):
        return jnp.where(h < MLA_HEADS, h, MLA_HEADS + h % 2)

    def value_index(h):
        return jnp.minimum(h // 2, MLA_HEADS // 2)

    def stage(h_sm, m_sm, s_sm, h_qk, s_qk, qk_ref=q_ref):
        if h_qk is not None:
            q = qk_ref[0, h_qk]
            ki = key_index(h_qk)
            m8 = jnp.full((8, tq), -jnp.inf, F32)
        if h_sm is not None:
            vi = value_index(h_sm)
            l8 = jnp.zeros((8, tq), F32)
            o = jnp.zeros((LANES, tq), F32)
        for src, lo, size, row in chunks:
            if h_qk is not None:
                st = lax.dot_general(k_refs[src][0, ki, lo:lo + size, :], q,
                                     (((1,), (1,)), ((), ())),
                                     preferred_element_type=F32)
                s_qk[row:row + size, :] = st
                m8 = jnp.maximum(m8, fold8(st, jnp.max))
            if h_sm is not None:
                pt = jnp.exp2(s_sm[row:row + size, :] - m_sm)
                l8 = l8 + fold8(pt, jnp.sum)
                o = o + jnp.dot(v_refs[src][0, vi, :, lo:lo + size], pt.astype(BF16),
                                preferred_element_type=F32)
        out = o / jnp.sum(l8, axis=0, keepdims=True) if h_sm is not None else None
        m_new = jnp.max(m8, axis=0, keepdims=True) if h_qk is not None else None
        return out, m_new

    row_id = lax.broadcasted_iota(jnp.int32, (LANES, tq), 0)

    def emit(j, oa, ob):
        o_ref[0, j] = jnp.where(row_id < 64, oa, ob).T.astype(BF16)

    def pair(j, m_a):
        oa, m_b = stage(2 * j, m_a, s0_ref, 2 * j + 1, s1_ref)
        ob, m_next = stage(2 * j + 1, m_b, s1_ref, 2 * j + 2, s0_ref)
        emit(j, oa, ob)
        return m_next

    i = pl.program_id(1)
    last_tile = pl.num_programs(1) - 1

    @pl.when(i == 0)
    def _():
        _, m0 = stage(None, None, None, 0, s0_ref)
        m_ref[...] = m0

    last = N_PAIRS - 1
    m_last = lax.fori_loop(0, last, pair, m_ref[...])
    oa, m_b = stage(2 * last, m_last, s0_ref, 2 * last + 1, s1_ref)

    @pl.when(i < last_tile)
    def _():
        ob, m_next = stage(2 * last + 1, m_b, s1_ref, 0, s0_ref, qn_ref)
        m_ref[...] = m_next
        emit(last, oa, ob)

    @pl.when(i == last_tile)
    def _():
        ob, _ = stage(2 * last + 1, m_b, s1_ref, None, None)
        emit(last, oa, ob)


def _attention(q, ks, vs, tq):
    batch, _, rows, _ = q.shape
    key_counts = tuple(k.shape[2] for k in ks)
    n_keys = sum(key_counts)
    n_tiles = rows // tq
    return pl.pallas_call(
        functools.partial(_attn_kernel, key_counts=key_counts, tq=tq),
        grid=(batch, n_tiles),
        in_specs=[pl.BlockSpec((1, N_QH, tq, LANES), lambda b, i: (b, 0, i, 0)),
                  pl.BlockSpec((1, 1, tq, LANES),
                               lambda b, i: (b, 0, jnp.minimum(i + 1, n_tiles - 1), 0))]
        + [pl.BlockSpec((1, N_KH, n, LANES), lambda b, i: (b, 0, 0, 0)) for n in key_counts]
        + [pl.BlockSpec((1, N_VH, LANES, n), lambda b, i: (b, 0, 0, 0)) for n in key_counts],
        out_specs=pl.BlockSpec((1, N_PAIRS, tq, LANES), lambda b, i: (b, 0, i, 0)),
        out_shape=jax.ShapeDtypeStruct((batch, N_PAIRS, rows, LANES), BF16),
        scratch_shapes=[pltpu.VMEM((n_keys, tq), F32), pltpu.VMEM((n_keys, tq), F32),
                        pltpu.VMEM((1, tq), F32)],
        compiler_params=pltpu.CompilerParams(
            dimension_semantics=("arbitrary", "arbitrary"), vmem_limit_bytes=VMEM_LIMIT),
        name="attention_%d" % n_keys,
    )(q, q, *ks, *vs)


def _out_proj_kernel(o_ref, glu_ref, prev_ref, next_ref, sg_ref, x_ref, mod_ref,
                     dww_ref, dwb_ref, lnw_ref, lnb_ref, pww_ref, pwb_ref, wout_ref, fnw_ref,
                     out_ref, ext_ref, sh_ref, *, chains, final):
    i = pl.program_id(0)
    rows_all = chains * TS
    ext_ref[0:HALO_ROWS, :] = jnp.where(i != 0, prev_ref[0], 0.0)
    ext_ref[HALO_ROWS:HALO_ROWS + rows_all, :] = glu_ref[0]
    ext_ref[HALO_ROWS + rows_all:, :] = jnp.where(i != pl.num_programs(0) - 1, next_ref[0], 0.0)

    for ph in range(8):
        sh_ref[ph] = ext_ref[ph:ph + rows_all + CONV_PAD, :]

    def chain(r0):
        rows = slice(r0, r0 + TS)
        acc = jnp.zeros((TS, CONV_CH), F32) + dwb_ref[...]
        for k in range(CONV_K):
            lo = HALO_ROWS - CONV_K // 2 + k
            base = r0 + lo - lo % 8
            acc = acc + sh_ref[lo % 8, base:base + TS, :] * dww_ref[k:k + 1, :]
        mu = jnp.mean(acc, axis=-1, keepdims=True)
        cen = acc - mu
        var = jnp.mean(cen * cen, axis=-1, keepdims=True)
        z = cen * lax.rsqrt(var + EPS) * lnw_ref[...] + lnb_ref[...]
        z = z * jax.nn.sigmoid(z)
        o_conv = jnp.dot(z.astype(BF16), pww_ref[...], preferred_element_type=F32) + pwb_ref[...]

        mix = jnp.concatenate([o_ref[0, j, rows, :].astype(F32) for j in range(N_PAIRS)] + [o_conv],
                              axis=1)
        y = (mix * sg_ref[0, rows, :].astype(F32)).astype(BF16)
        r = jnp.dot(y, wout_ref[...], preferred_element_type=F32)
        x_new = x_ref[0, rows, :] + mod_ref[0, :, 2 * D_MODEL:] * r
        if final:
            x_new = _rms(x_new, fnw_ref[...], D_MODEL)
        out_ref[0, rows, :] = x_new

    for c in range(chains):
        chain(c * TS)


def _out_proj(o, glu, sg, x, mod_l, mod_row, chains, dww, dwb, lnw, lnb, pww, pwb, wout, fnw, final):
    batch, rows, _ = x.shape
    tsz = chains * TS
    assert rows % tsz == 0
    hb = tsz // HALO_ROWS
    last_halo = rows // HALO_ROWS - 1

    def const(shape):
        return pl.BlockSpec(shape, lambda i, b: (0,) * len(shape))

    tile = lambda w: pl.BlockSpec((1, tsz, w), lambda i, b: (b, i, 0))
    return pl.pallas_call(
        functools.partial(_out_proj_kernel, chains=chains, final=final),
        grid=(rows // tsz, batch),
        in_specs=[
            pl.BlockSpec((1, N_PAIRS, tsz, LANES), lambda i, b: (b, 0, i, 0)),
            tile(CONV_CH),
            pl.BlockSpec((1, HALO_ROWS, CONV_CH), lambda i, b: (b, jnp.maximum(i * hb - 1, 0), 0)),
            pl.BlockSpec((1, HALO_ROWS, CONV_CH),
                         lambda i, b: (b, jnp.minimum((i + 1) * hb, last_halo), 0)),
            tile(MIX_WIDTH), tile(D_MODEL),
            pl.BlockSpec((1, 1, 3 * D_MODEL),
                         lambda i, b: (b if mod_row is None else mod_row, 0, 0)),
            const((CONV_K, CONV_CH)), const((1, CONV_CH)), const((1, CONV_CH)), const((1, CONV_CH)),
            const((CONV_CH, CONV_CH)), const((1, CONV_CH)), const((MIX_WIDTH, D_MODEL)),
            const((1, D_MODEL)),
        ],
        out_specs=tile(D_MODEL),
        out_shape=jax.ShapeDtypeStruct((batch, rows, D_MODEL), F32),
        scratch_shapes=[pltpu.VMEM((tsz + 2 * HALO_ROWS, CONV_CH), F32),
                        pltpu.VMEM((8, tsz + CONV_PAD, CONV_CH), F32)],
        compiler_params=pltpu.CompilerParams(
            dimension_semantics=("arbitrary", "arbitrary"), vmem_limit_bytes=VMEM_LIMIT),
        name="out_proj_%d%s" % (chains, "_final" if final else ""),
    )(o, glu, glu, glu, sg, x, mod_l, dww, dwb, lnw, lnb, pww, pwb, wout, fnw)


def kernel(x, c, ctx, c_ctx, norm_w, w_mod, b_mod, w_in, mla_q_norm, mla_w_uq, mla_kv_norm,
           mla_w_ukv, gqa_q_norm, gqa_k_norm, conv_dw_w, conv_dw_b, conv_ln_w, conv_ln_b,
           conv_pw_w, conv_pw_b, w_out, final_norm_w):
    batch, seq, _ = x.shape
    ctx_len = ctx.shape[1]
    depth = w_in.shape[0]
    assert seq % (LAT_CHAINS * TS) == 0 and seq % TQ == 0 and seq % GRID_W == 0
    assert ctx_len == TS
    mod_rows = 16
    assert batch < mod_rows

    c_all = jnp.concatenate([c, c_ctx[None, :], jnp.zeros((mod_rows - batch - 1, D_MODEL), F32)], 0)
    mod = _modulation(c_all, w_mod, b_mod)
    tab_lat = _rope_tables(seq)
    tab_ctx = _identity_tables(ctx_len)
    row = lambda v: v.reshape(1, -1)
    dup = lambda v: jnp.concatenate([v, v]).reshape(1, LANES)

    for l in range(depth):
        final = l == depth - 1
        mod_l = mod[l].reshape(mod_rows, 1, 3 * D_MODEL)
        weights = (_layout_w_in(w_in[l]), _layout_w_uq(mla_w_uq[l]), _layout_w_ukv(mla_w_ukv[l]),
                   row(mla_q_norm[l]), row(mla_kv_norm[l]), dup(gqa_q_norm[l]), dup(gqa_k_norm[l]))
        q_lat, k_lat, v_lat, glu_lat, sg_lat = _in_proj(
            x, mod_l, None, LAT_CHAINS, False, row(norm_w[l]), tab_lat, *weights)
        if final:
            k_ctx, v_ctx = _in_proj(ctx, mod_l, batch, 1, True, row(norm_w[l]), tab_ctx, *weights)
        else:
            q_ctx, k_ctx, v_ctx, glu_ctx, sg_ctx = _in_proj(
                ctx, mod_l, batch, 1, False, row(norm_w[l]), tab_ctx, *weights)
        conv = (conv_dw_w[l], row(conv_dw_b[l]), row(conv_ln_w[l]), row(conv_ln_b[l]),
                conv_pw_w[l].astype(BF16), row(conv_pw_b[l]),
                _mix_order(w_out[l]).astype(BF16), row(final_norm_w))
        o_lat = _attention(q_lat, (k_lat, k_ctx), (v_lat, v_ctx), TQ)
        if not final:
            o_ctx = _attention(q_ctx, (k_ctx,), (v_ctx,), ctx_len)
            ctx = _out_proj(o_ctx, glu_ctx, sg_ctx, ctx, mod_l, batch, 1, *conv, False)
        x = _out_proj(o_lat, glu_lat, sg_lat, x, mod_l, None, LAT_CHAINS, *conv, final)
    return x
```

```python
import functools
import math

import jax
import jax.numpy as jnp
from jax import lax
from jax.experimental import pallas as pl
from jax.experimental.pallas import tpu as pltpu

F32 = jnp.float32
BF16 = jnp.bfloat16

LANES = 128
HALO_ROWS = 16
VMEM_LIMIT = 56 * 1024 * 1024

D_MODEL = 1024
GRID_W = 64
ROPE_THETA = 10000.0
EPS = 1e-6
LOG2E = math.log2(math.e)

MLA_HEADS = 6
MLA_Q_RANK = 256
MLA_KV_RANK = 128
MLA_NOPE = 64
MLA_ROPE = 32
GQA_HEADS = 6
GQA_KV_HEADS = 2
GQA_HEAD_DIM = 64
CONV_CH = 256
CONV_K = 31
MIX_WIDTH = 1024

OFF_MLA_Q = 0
OFF_KV = 256
OFF_KR = OFF_KV + MLA_KV_RANK
OFF_GQA_K = OFF_KR + MLA_ROPE
OFF_GQA_V = OFF_GQA_K + 128
OFF_GQA_Q = OFF_GQA_V + 128
OFF_CONV = OFF_GQA_Q + 384
OFF_GATE = OFF_CONV + 2 * CONV_CH

P_CQ = 0
P_KV = 256
P_GK = 512
P_GQ = 768
P_CONV = 1152
P_GATE = 1664
P_COLS = 2688

GQA_SLOTS = (0, 3, 1, 4, 2, 5)

TS = 256
N_QH = 12
N_KH = 8
N_VH = 4
N_PAIRS = 6
TQ = 512
KEY_CHUNK = 512
CONV_PAD = 24
MOD_ROWS = 16
LAT_CHAINS = 4


def _pad_cols(w, width):
    return jnp.pad(w, [(0, 0)] * (w.ndim - 1) + [(0, width - w.shape[-1])])


def _layout_w_in(w_in):
    gate = w_in[..., OFF_GATE:]
    zeros = lambda n: jnp.zeros(w_in.shape[:-1] + (n,), F32)
    cols = [w_in[..., :OFF_KR], zeros(64), w_in[..., OFF_KR:OFF_GQA_K], zeros(32),
            w_in[..., OFF_GQA_K:OFF_GQA_Q]]
    cols += [w_in[..., OFF_GQA_Q + 64 * h:OFF_GQA_Q + 64 * (h + 1)] for h in GQA_SLOTS]
    cols += [w_in[..., OFF_CONV:OFF_GATE], gate[..., :384]]
    cols += [gate[..., 384 + 64 * h:384 + 64 * (h + 1)] for h in GQA_SLOTS]
    cols.append(gate[..., 768:])
    out = jnp.concatenate(cols, axis=-1)
    assert out.shape[-1] == P_COLS
    return out.astype(BF16)


def _mix_order(w_out):
    gqa = [w_out[..., 384 + 64 * h:384 + 64 * (h + 1), :] for h in GQA_SLOTS]
    return jnp.concatenate([w_out[..., :384, :]] + gqa + [w_out[..., 768:, :]], axis=-2)


def _layout_w_uq(w_uq):
    cols = [_pad_cols(w_uq[..., 96 * h:96 * (h + 1)], LANES) for h in range(MLA_HEADS)]
    return jnp.concatenate(cols, axis=-1).astype(BF16)


def _layout_w_ukv(w_ukv):
    k = [_pad_cols(w_ukv[..., 128 * h:128 * h + 64], LANES) for h in range(MLA_HEADS)]
    v = [w_ukv[..., 128 * h + 64:128 * (h + 1)] for h in range(MLA_HEADS)]
    return jnp.concatenate(k + v, axis=-1).astype(BF16)


Q_SCALE_MLA = (MLA_NOPE + MLA_ROPE) ** -0.5 * LOG2E
Q_SCALE_GQA = GQA_HEAD_DIM ** -0.5 * LOG2E


def _rope_tables(seq):
    t = jnp.arange(seq, dtype=jnp.int32)
    row = (t // GRID_W).astype(F32)[:, None]
    col = (t % GRID_W).astype(F32)[:, None]

    def seg_tables(width):
        r = width // 2
        half = r // 2
        freqs = ROPE_THETA ** (-jnp.arange(half, dtype=F32) / half)
        cs, ss = [], []
        for pos in (row, col):
            ang = pos * freqs[None, :]
            c, s = jnp.cos(ang), jnp.sin(ang)
            cs += [c, c]
            ss += [-s, s]
        return jnp.concatenate(cs, axis=1), jnp.concatenate(ss, axis=1)

    def full(c, s, lo):
        w = c.shape[1]
        cf = jnp.concatenate([jnp.ones((seq, lo), F32), c, jnp.ones((seq, LANES - lo - w), F32)], 1)
        sf = jnp.concatenate([jnp.zeros((seq, lo), F32), s, jnp.zeros((seq, LANES - lo - w), F32)], 1)
        return cf, sf

    mc, ms = full(*seg_tables(MLA_ROPE), MLA_NOPE)
    gc, gs = seg_tables(GQA_HEAD_DIM)
    gc, gs = full(jnp.tile(gc, (1, 2)), jnp.tile(gs, (1, 2)), 0)
    return jnp.stack([mc * Q_SCALE_MLA, ms * Q_SCALE_MLA, mc, ms,
                      gc * Q_SCALE_GQA, gs * Q_SCALE_GQA, gc, gs])


def _identity_tables(rows):
    one = jnp.ones((rows, LANES), F32)
    zero = jnp.zeros((rows, LANES), F32)
    return jnp.stack([one * Q_SCALE_MLA, zero, one, zero, one * Q_SCALE_GQA, zero, one, zero])


def _mod_kernel(c_ref, w_ref, b_ref, o_ref):
    a = c_ref[...]
    a = a * jax.nn.sigmoid(a)
    o_ref[0] = jnp.dot(a, w_ref[0], preferred_element_type=F32,
                       precision=lax.Precision.HIGHEST) + b_ref[0]


def _modulation(c_all, w_mod, b_mod):
    depth = w_mod.shape[0]
    rows = c_all.shape[0]
    tn = 1024
    return pl.pallas_call(
        _mod_kernel,
        grid=(depth, 3 * D_MODEL // tn),
        in_specs=[pl.BlockSpec((rows, D_MODEL), lambda l, n: (0, 0)),
                  pl.BlockSpec((1, D_MODEL, tn), lambda l, n: (l, 0, n)),
                  pl.BlockSpec((1, 1, tn), lambda l, n: (l, 0, n))],
        out_specs=pl.BlockSpec((1, rows, tn), lambda l, n: (l, 0, n)),
        out_shape=jax.ShapeDtypeStruct((depth, rows, 3 * D_MODEL), F32),
        compiler_params=pltpu.CompilerParams(
            dimension_semantics=("arbitrary", "arbitrary"), vmem_limit_bytes=VMEM_LIMIT),
        name="adaln_mod",
    )(c_all, w_mod, b_mod.reshape(depth, 1, 3 * D_MODEL))


def _rms(x, gain, n):
    ss = jnp.sum(x * x, axis=-1, keepdims=True) * (1.0 / n)
    return x * lax.rsqrt(ss + EPS) * gain


def _rope(x, cos, sin, half):
    lane = lax.broadcasted_iota(jnp.int32, x.shape, 1)
    first = (lane % (2 * half)) < half
    partner = jnp.where(first, pltpu.roll(x, LANES - half, 1), pltpu.roll(x, half, 1))
    return x * cos + partner * sin


def _rms_halves(x, gain):
    low = lax.broadcasted_iota(jnp.int32, x.shape, 1) < 64
    x2 = x * x
    s_lo = jnp.sum(jnp.where(low, x2, 0.0), axis=-1, keepdims=True)
    s_hi = jnp.sum(jnp.where(low, 0.0, x2), axis=-1, keepdims=True)
    ss = jnp.where(low, s_lo, s_hi) * (1.0 / 64)
    return x * lax.rsqrt(ss + EPS) * gain


def _split_halves(x):
    low = lax.broadcasted_iota(jnp.int32, x.shape, 1) < 64
    return jnp.where(low, x, 0.0), jnp.where(low, 0.0, x)


def _in_proj_kernel(x_ref, mod_ref, nw_ref, tab_ref, win_ref, wuq_ref, wukv_ref,
                    qn_ref, kvn_ref, gqn_ref, gkn_ref, *out_refs, chains, kv_only):
    if kv_only:
        k_ref, v_ref = out_refs
    else:
        q_ref, k_ref, v_ref, glu_ref, sg_ref = out_refs
    shift = mod_ref[0, :, 0:D_MODEL]
    scale = mod_ref[0, :, D_MODEL:2 * D_MODEL]
    rows = [slice(c * TS, (c + 1) * TS) for c in range(chains)]
    each = range(chains)

    def tab(t, c):
        return tab_ref[t, rows[c], :]

    hb, pcq, pkv, pq, pg = [], [], [], [], []
    for c in each:
        h = _rms(x_ref[0, rows[c], :], nw_ref[...], D_MODEL) * (1.0 + scale) + shift
        hb.append(h.astype(BF16))
        proj = lambda lo, hi, c=c: jnp.dot(hb[c], win_ref[:, lo:hi], preferred_element_type=F32)
        pkv.append(proj(P_KV, P_GK))
        pg.append(proj(P_GK, P_GQ))
        if not kv_only:
            pcq.append(proj(P_CQ, P_KV))
            pq.append(proj(P_GQ, P_CONV))

    def proj(c, lo, hi):
        return jnp.dot(hb[c], win_ref[:, lo:hi], preferred_element_type=F32)

    q, kv, pc = [], [], []
    for c in each:
        kv.append(jnp.dot(_rms(pkv[c][:, :LANES], kvn_ref[...], MLA_KV_RANK).astype(BF16),
                          wukv_ref[...], preferred_element_type=F32))
        if not kv_only:
            q.append(jnp.dot(_rms(pcq[c], qn_ref[...], MLA_Q_RANK).astype(BF16), wuq_ref[...],
                             preferred_element_type=F32))
            pc.append(proj(c, P_CONV, P_GATE))

    for c in each if not kv_only else ():
        for j in range(GQA_HEADS // 2):
            qq = _rms_halves(pq[c][:, LANES * j:LANES * (j + 1)], gqn_ref[...])
            qa, qb = _split_halves(_rope(qq, tab(4, c), tab(5, c), GQA_HEAD_DIM // 4))
            q_ref[0, MLA_HEADS + 2 * j, rows[c], :] = qa.astype(BF16)
            q_ref[0, MLA_HEADS + 2 * j + 1, rows[c], :] = qb.astype(BF16)
        for hd in range(MLA_HEADS):
            g = _rope(q[c][:, LANES * hd:LANES * (hd + 1)], tab(0, c), tab(1, c), MLA_ROPE // 4)
            q_ref[0, hd, rows[c], :] = g.astype(BF16)

    gt = [proj(c, P_GATE, P_COLS) for c in each] if not kv_only else None

    for c in each:
        k_rope = _rope(pkv[c][:, LANES:], tab(2, c), tab(3, c), MLA_ROPE // 4)
        for hd in range(MLA_HEADS):
            k_ref[0, hd, rows[c], :] = (kv[c][:, LANES * hd:LANES * (hd + 1)] + k_rope).astype(BF16)
        for j in range(MLA_HEADS // 2):
            lo = LANES * (MLA_HEADS + j)
            v_ref[0, j, :, rows[c]] = kv[c][:, lo:lo + LANES].T.astype(BF16)
        kk = _rms_halves(pg[c][:, :LANES], gkn_ref[...])
        ka, kb = _split_halves(_rope(kk, tab(6, c), tab(7, c), GQA_HEAD_DIM // 4))
        k_ref[0, MLA_HEADS, rows[c], :] = ka.astype(BF16)
        k_ref[0, MLA_HEADS + 1, rows[c], :] = kb.astype(BF16)
        v_ref[0, MLA_HEADS // 2, :, rows[c]] = pg[c][:, LANES:].T.astype(BF16)
        if not kv_only:
            glu_ref[0, rows[c], :] = pc[c][:, :CONV_CH] * jax.nn.sigmoid(pc[c][:, CONV_CH:])
            sg_ref[0, rows[c], :] = (gt[c] * jax.nn.sigmoid(gt[c])).astype(BF16)


def _in_proj(x, layer, mod, mod_row, chains, kv_only, nw, tab, win, wuq, wukv, qn, kvn, gqn, gkn):
    batch, rows, _ = x.shape
    tsz = chains * TS
    assert rows % tsz == 0

    def const(shape):
        return pl.BlockSpec((None,) + shape, lambda i, b: (layer,) + (0,) * len(shape))

    def heads(n):
        return pl.BlockSpec((1, n, tsz, LANES), lambda i, b: (b, 0, i, 0))

    out_specs = [heads(N_QH), heads(N_KH),
                 pl.BlockSpec((1, N_VH, LANES, tsz), lambda i, b: (b, 0, 0, i)),
                 pl.BlockSpec((1, tsz, CONV_CH), lambda i, b: (b, i, 0)),
                 pl.BlockSpec((1, tsz, MIX_WIDTH), lambda i, b: (b, i, 0))]
    out_shape = [jax.ShapeDtypeStruct((batch, N_QH, rows, LANES), BF16),
                 jax.ShapeDtypeStruct((batch, N_KH, rows, LANES), BF16),
                 jax.ShapeDtypeStruct((batch, N_VH, LANES, rows), BF16),
                 jax.ShapeDtypeStruct((batch, rows, CONV_CH), F32),
                 jax.ShapeDtypeStruct((batch, rows, MIX_WIDTH), BF16)]
    return pl.pallas_call(
        functools.partial(_in_proj_kernel, chains=chains, kv_only=kv_only),
        grid=(rows // tsz, batch),
        in_specs=[
            pl.BlockSpec((1, tsz, D_MODEL), lambda i, b: (b, i, 0)),
            pl.BlockSpec((1, 1, 3 * D_MODEL),
                         lambda i, b: (layer * MOD_ROWS + (b if mod_row is None else mod_row), 0, 0)),
            const((1, D_MODEL)),
            pl.BlockSpec((8, tsz, LANES), lambda i, b: (0, i, 0)),
            const((D_MODEL, P_COLS)), const((MLA_Q_RANK, 768)), const((MLA_KV_RANK, 1152)),
            const((1, MLA_Q_RANK)), const((1, MLA_KV_RANK)), const((1, LANES)), const((1, LANES)),
        ],
        out_specs=out_specs[1:3] if kv_only else out_specs,
        out_shape=out_shape[1:3] if kv_only else out_shape,
        compiler_params=pltpu.CompilerParams(
            dimension_semantics=("arbitrary", "arbitrary"), vmem_limit_bytes=VMEM_LIMIT),
        name="in_proj_%d%s" % (chains, "_kv" if kv_only else ""),
    )(x, mod, nw, tab, win, wuq, wukv, qn, kvn, gqn, gkn)


def _key_chunks(key_counts):
    chunks, row = [], 0
    for src, n in enumerate(key_counts):
        lo = 0
        while lo < n:
            size = min(KEY_CHUNK, n - lo)
            chunks.append((src, lo, size, row))
            lo += size
            row += size
    return chunks


def _attn_kernel(q_ref, qn_ref, *refs, key_counts, tq):
    n_src = len(key_counts)
    k_refs, v_refs = refs[:n_src], refs[n_src:2 * n_src]
    o_ref, s0_ref, s1_ref, m_ref = refs[2 * n_src:]
    chunks = _key_chunks(key_counts)

    def fold8(x, op):
        return op(x.reshape(x.shape[0] // 8, 8, tq), axis=0)

    def key_index(h):
        return jnp.where(h < MLA_HEADS, h, MLA_HEADS + h % 2)

    def value_index(h):
        return jnp.minimum(h // 2, MLA_HEADS // 2)

    def stage(h_sm, m_sm, s_sm, h_qk, s_qk, qk_ref=q_ref):
        if h_qk is not None:
            q = qk_ref[0, h_qk]
            ki = key_index(h_qk)
            m8 = jnp.full((8, tq), -jnp.inf, F32)
        if h_sm is not None:
            vi = value_index(h_sm)
            l8 = jnp.zeros((8, tq), F32)
            o = jnp.zeros((LANES, tq), F32)
        for src, lo, size, row in chunks:
            if h_qk is not None:
                st = lax.dot_general(k_refs[src][0, ki, lo:lo + size, :], q,
                                     (((1,), (1,)), ((), ())),
                                     preferred_element_type=F32)
                s_qk[row:row + size, :] = st
                m8 = jnp.maximum(m8, fold8(st, jnp.max))
            if h_sm is not None:
                pt = jnp.exp2(s_sm[row:row + size, :] - m_sm)
                l8 = l8 + fold8(pt, jnp.sum)
                o = o + jnp.dot(v_refs[src][0, vi, :, lo:lo + size], pt.astype(BF16),
                                preferred_element_type=F32)
        out = o / jnp.sum(l8, axis=0, keepdims=True) if h_sm is not None else None
        m_new = jnp.max(m8, axis=0, keepdims=True) if h_qk is not None else None
        return out, m_new

    row_id = lax.broadcasted_iota(jnp.int32, (LANES, tq), 0)

    def emit(j, oa, ob):
        o_ref[0, j] = jnp.where(row_id < 64, oa, ob).T.astype(BF16)

    def pair(j, m_a):
        oa, m_b = stage(2 * j, m_a, s0_ref, 2 * j + 1, s1_ref)
        ob, m_next = stage(2 * j + 1, m_b, s1_ref, 2 * j + 2, s0_ref)
        emit(j, oa, ob)
        return m_next

    i = pl.program_id(1)
    last_tile = pl.num_programs(1) - 1

    @pl.when(i == 0)
    def _():
        _, m0 = stage(None, None, None, 0, s0_ref)
        m_ref[...] = m0

    last = N_PAIRS - 1
    m_last = lax.fori_loop(0, last, pair, m_ref[...])
    oa, m_b = stage(2 * last, m_last, s0_ref, 2 * last + 1, s1_ref)

    @pl.when(i < last_tile)
    def _():
        ob, m_next = stage(2 * last + 1, m_b, s1_ref, 0, s0_ref, qn_ref)
        m_ref[...] = m_next
        emit(last, oa, ob)

    @pl.when(i == last_tile)
    def _():
        ob, _ = stage(2 * last + 1, m_b, s1_ref, None, None)
        emit(last, oa, ob)


def _attention(q, ks, vs, tq):
    batch, _, rows, _ = q.shape
    key_counts = tuple(k.shape[2] for k in ks)
    n_keys = sum(key_counts)
    n_tiles = rows // tq
    return pl.pallas_call(
        functools.partial(_attn_kernel, key_counts=key_counts, tq=tq),
        grid=(batch, n_tiles),
        in_specs=[pl.BlockSpec((1, N_QH, tq, LANES), lambda b, i: (b, 0, i, 0)),
                  pl.BlockSpec((1, 1, tq, LANES),
                               lambda b, i: (b, 0, jnp.minimum(i + 1, n_tiles - 1), 0))]
        + [pl.BlockSpec((1, N_KH, n, LANES), lambda b, i: (b, 0, 0, 0)) for n in key_counts]
        + [pl.BlockSpec((1, N_VH, LANES, n), lambda b, i: (b, 0, 0, 0)) for n in key_counts],
        out_specs=pl.BlockSpec((1, N_PAIRS, tq, LANES), lambda b, i: (b, 0, i, 0)),
        out_shape=jax.ShapeDtypeStruct((batch, N_PAIRS, rows, LANES), BF16),
        scratch_shapes=[pltpu.VMEM((n_keys, tq), F32), pltpu.VMEM((n_keys, tq), F32),
                        pltpu.VMEM((1, tq), F32)],
        compiler_params=pltpu.CompilerParams(
            dimension_semantics=("arbitrary", "arbitrary"), vmem_limit_bytes=VMEM_LIMIT),
        name="attention_%d" % n_keys,
    )(q, q, *ks, *vs)


def _out_proj_kernel(o_ref, glu_ref, prev_ref, next_ref, sg_ref, x_ref, mod_ref,
                     dww_ref, dwb_ref, lnw_ref, lnb_ref, pww_ref, pwb_ref, wout_ref, fnw_ref,
                     out_ref, ext_ref, sh_ref, *, chains, final):
    i = pl.program_id(0)
    rows_all = chains * TS
    ext_ref[0:HALO_ROWS, :] = jnp.where(i != 0, prev_ref[0], 0.0)
    ext_ref[HALO_ROWS:HALO_ROWS + rows_all, :] = glu_ref[0]
    ext_ref[HALO_ROWS + rows_all:, :] = jnp.where(i != pl.num_programs(0) - 1, next_ref[0], 0.0)

    for ph in range(8):
        sh_ref[ph] = ext_ref[ph:ph + rows_all + CONV_PAD, :]

    def chain(r0):
        rows = slice(r0, r0 + TS)
        acc = jnp.zeros((TS, CONV_CH), F32) + dwb_ref[...]
        for k in range(CONV_K):
            lo = HALO_ROWS - CONV_K // 2 + k
            base = r0 + lo - lo % 8
            acc = acc + sh_ref[lo % 8, base:base + TS, :] * dww_ref[k:k + 1, :]
        mu = jnp.mean(acc, axis=-1, keepdims=True)
        cen = acc - mu
        var = jnp.mean(cen * cen, axis=-1, keepdims=True)
        z = cen * lax.rsqrt(var + EPS) * lnw_ref[...] + lnb_ref[...]
        z = z * jax.nn.sigmoid(z)
        o_conv = jnp.dot(z.astype(BF16), pww_ref[...], preferred_element_type=F32) + pwb_ref[...]

        mix = jnp.concatenate([o_ref[0, j, rows, :].astype(F32) for j in range(N_PAIRS)] + [o_conv],
                              axis=1)
        y = (mix * sg_ref[0, rows, :].astype(F32)).astype(BF16)
        r = jnp.dot(y, wout_ref[...], preferred_element_type=F32)
        x_new = x_ref[0, rows, :] + mod_ref[0, :, 2 * D_MODEL:] * r
        if final:
            x_new = _rms(x_new, fnw_ref[...], D_MODEL)
        out_ref[0, rows, :] = x_new

    for c in range(chains):
        chain(c * TS)


def _out_proj(o, glu, sg, x, layer, mod, mod_row, chains, dww, dwb, lnw, lnb, pww, pwb, wout, fnw,
              final):
    batch, rows, _ = x.shape
    tsz = chains * TS
    assert rows % tsz == 0
    hb = tsz // HALO_ROWS
    last_halo = rows // HALO_ROWS - 1

    def const(shape):
        return pl.BlockSpec((None,) + shape, lambda i, b: (layer,) + (0,) * len(shape))

    tile = lambda w: pl.BlockSpec((1, tsz, w), lambda i, b: (b, i, 0))
    return pl.pallas_call(
        functools.partial(_out_proj_kernel, chains=chains, final=final),
        grid=(rows // tsz, batch),
        in_specs=[
            pl.BlockSpec((1, N_PAIRS, tsz, LANES), lambda i, b: (b, 0, i, 0)),
            tile(CONV_CH),
            pl.BlockSpec((1, HALO_ROWS, CONV_CH), lambda i, b: (b, jnp.maximum(i * hb - 1, 0), 0)),
            pl.BlockSpec((1, HALO_ROWS, CONV_CH),
                         lambda i, b: (b, jnp.minimum((i + 1) * hb, last_halo), 0)),
            tile(MIX_WIDTH), tile(D_MODEL),
            pl.BlockSpec((1, 1, 3 * D_MODEL),
                         lambda i, b: (layer * MOD_ROWS + (b if mod_row is None else mod_row), 0, 0)),
            const((CONV_K, CONV_CH)), const((1, CONV_CH)), const((1, CONV_CH)), const((1, CONV_CH)),
            const((CONV_CH, CONV_CH)), const((1, CONV_CH)), const((MIX_WIDTH, D_MODEL)),
            pl.BlockSpec((1, D_MODEL), lambda i, b: (0, 0)),
        ],
        out_specs=tile(D_MODEL),
        out_shape=jax.ShapeDtypeStruct((batch, rows, D_MODEL), F32),
        scratch_shapes=[pltpu.VMEM((tsz + 2 * HALO_ROWS, CONV_CH), F32),
                        pltpu.VMEM((8, tsz + CONV_PAD, CONV_CH), F32)],
        compiler_params=pltpu.CompilerParams(
            dimension_semantics=("arbitrary", "arbitrary"), vmem_limit_bytes=VMEM_LIMIT),
        name="out_proj_%d%s" % (chains, "_final" if final else ""),
    )(o, glu, glu, glu, sg, x, mod, dww, dwb, lnw, lnb, pww, pwb, wout, fnw)


def kernel(x, c, ctx, c_ctx, norm_w, w_mod, b_mod, w_in, mla_q_norm, mla_w_uq, mla_kv_norm,
           mla_w_ukv, gqa_q_norm, gqa_k_norm, conv_dw_w, conv_dw_b, conv_ln_w, conv_ln_b,
           conv_pw_w, conv_pw_b, w_out, final_norm_w):
    batch, seq, _ = x.shape
    ctx_len = ctx.shape[1]
    depth = w_in.shape[0]
    assert seq % (LAT_CHAINS * TS) == 0 and seq % TQ == 0 and seq % GRID_W == 0
    assert ctx_len == TS
    assert batch < MOD_ROWS

    c_all = jnp.concatenate([c, c_ctx[None, :], jnp.zeros((MOD_ROWS - batch - 1, D_MODEL), F32)], 0)
    mod = _modulation(c_all, w_mod, b_mod)
    mod = mod.reshape(depth * MOD_ROWS, 1, 3 * D_MODEL)
    tab_lat = _rope_tables(seq)
    tab_ctx = _identity_tables(ctx_len)
    vec = lambda v: v[:, None, :]
    dup = lambda v: jnp.concatenate([v, v], axis=-1)[:, None, :]
    proj_w = (_layout_w_in(w_in), _layout_w_uq(mla_w_uq), _layout_w_ukv(mla_w_ukv),
              vec(mla_q_norm), vec(mla_kv_norm), dup(gqa_q_norm), dup(gqa_k_norm))
    conv_w = (conv_dw_w, vec(conv_dw_b), vec(conv_ln_w), vec(conv_ln_b), conv_pw_w.astype(BF16),
              vec(conv_pw_b), _mix_order(w_out).astype(BF16), final_norm_w.reshape(1, -1))

    def in_proj(tokens, l, mod_row, chains, kv_only, tab):
        return _in_proj(tokens, l, mod, mod_row, chains, kv_only, vec(norm_w), tab, *proj_w)

    for l in range(depth):
        final = l == depth - 1
        q_lat, k_lat, v_lat, glu_lat, sg_lat = in_proj(x, l, None, LAT_CHAINS, False, tab_lat)
        if final:
            k_ctx, v_ctx = in_proj(ctx, l, batch, 1, True, tab_ctx)
        else:
            q_ctx, k_ctx, v_ctx, glu_ctx, sg_ctx = in_proj(ctx, l, batch, 1, False, tab_ctx)
        o_lat = _attention(q_lat, (k_lat, k_ctx), (v_lat, v_ctx), TQ)
        if not final:
            o_ctx = _attention(q_ctx, (k_ctx,), (v_ctx,), ctx_len)
            ctx = _out_proj(o_ctx, glu_ctx, sg_ctx, ctx, l, mod, batch, 1, *conv_w, False)
        x = _out_proj(o_lat, glu_lat, sg_lat, x, l, mod, None, LAT_CHAINS, *conv_w, final)
    return x
```
